```python
import jax, jax.numpy as jnp
from jax import lax
import numpy as np

D_MODEL = 1024
BATCH = 4
SEQ = 8192
DEPTH = 2

MLA_HEADS = 6
MLA_NOPE = 64
MLA_ROPE = 32
MLA_V = 64
MLA_Q_RANK = 256
MLA_KV_RANK = 128
MLA_WIDTH = MLA_HEADS * MLA_V

LRU_WIDTH = 384
LRU_BLOCKS = 6
LRU_BLOCK = LRU_WIDTH // LRU_BLOCKS
LRU_C = 8.0
CONV_WIDTH = 4
CONV_PAD = (1, 2)

SWA_Q_HEADS = 4
SWA_KV_HEADS = 2
SWA_HEAD_DIM = 64
SWA_WINDOW = 128
SWA_WIDTH = SWA_Q_HEADS * SWA_HEAD_DIM

MIX_WIDTH = MLA_WIDTH + LRU_WIDTH + SWA_WIDTH
BLOCK_Q = 128
ROPE_THETA = 10000.0
EPS = 1e-6
NEG_BIG = -1e30

IN_SPLITS = (MLA_Q_RANK, MLA_KV_RANK, MLA_ROPE,
             SWA_WIDTH, SWA_KV_HEADS * SWA_HEAD_DIM, SWA_KV_HEADS * SWA_HEAD_DIM,
             LRU_WIDTH, LRU_WIDTH)
IN_WIDTH = 1696

PEER_HEADS = 8
PEER_NKEYS = 128
PEER_EXPERTS = PEER_NKEYS * PEER_NKEYS
PEER_DKEY = 256
PEER_TOPK = 16
PEER_CHUNK = 128

kernel_name = 'hybrid_mla_rglru_swa_peer_encoder'


def rmsnorm(x, g):
    xf = x.astype(jnp.float32)
    y = xf * lax.rsqrt(jnp.mean(xf * xf, axis=-1, keepdims=True) + EPS)
    return (y * g.astype(jnp.float32)).astype(x.dtype)


def rope(x, positions):
    d = x.shape[-1]
    inv = ROPE_THETA ** (-jnp.arange(0, d, 2, dtype=jnp.float32) / d)
    ang = positions.astype(jnp.float32)[..., None] * inv
    cos = jnp.cos(ang)[:, :, None, :]
    sin = jnp.sin(ang)[:, :, None, :]
    xf = x.astype(jnp.float32)
    x1, x2 = xf[..., : d // 2], xf[..., d // 2:]
    return jnp.concatenate([x1 * cos - x2 * sin, x2 * cos + x1 * sin], axis=-1).astype(x.dtype)


def mla_mixer(c_q, c_kv, k_rope, positions, q_norm, w_uq, kv_norm, w_ukv):
    B, S, _ = c_q.shape
    q = (rmsnorm(c_q, q_norm) @ w_uq).reshape(B, S, MLA_HEADS, MLA_NOPE + MLA_ROPE)
    q_nope = q[..., :MLA_NOPE]
    q_rope = rope(q[..., MLA_NOPE:], positions)
    kv = (rmsnorm(c_kv, kv_norm) @ w_ukv).reshape(B, S, MLA_HEADS, MLA_NOPE + MLA_V)
    k_nope, v = kv[..., :MLA_NOPE], kv[..., MLA_NOPE:]
    k_r = rope(k_rope[:, :, None, :], positions)[:, :, 0]
    scale = (MLA_NOPE + MLA_ROPE) ** -0.5
    nb = S // BLOCK_Q

    def to_blocks(t):
        return jnp.moveaxis(t.reshape(B, nb, BLOCK_Q, *t.shape[2:]), 1, 0)

    def attend(qb):
        qn, qr = qb
        s = (jnp.einsum('bqhd,bkhd->bhqk', qn, k_nope, preferred_element_type=jnp.float32)
             + jnp.einsum('bqhr,bkr->bhqk', qr, k_r, preferred_element_type=jnp.float32))
        p = jax.nn.softmax(s * scale, axis=-1)
        return jnp.einsum('bhqk,bkhd->bqhd', p.astype(v.dtype), v)

    o = lax.map(attend, (to_blocks(q_nope), to_blocks(q_rope)))
    return jnp.moveaxis(o, 0, 1).reshape(B, S, MLA_WIDTH)


def _lin_combine(left, right):
    a_l, b_l = left
    a_r, b_r = right
    return a_l * a_r, a_r * b_l + b_r


def rglru_mixer(x_branch, gate_branch, conv_w, conv_b, wa, ba, wx, bx, lam):
    B, S, _ = x_branch.shape
    xc = lax.conv_general_dilated(
        x_branch, conv_w[:, None, :].astype(x_branch.dtype), window_strides=(1,),
        padding=[CONV_PAD], dimension_numbers=('NWC', 'WIO', 'NWC'),
        feature_group_count=LRU_WIDTH) + conv_b
    xf = xc.astype(jnp.float32)
    xb = xf.reshape(B, S, LRU_BLOCKS, LRU_BLOCK)

    def direction(d, reverse):
        r = jax.nn.sigmoid(jnp.einsum('bsni,nij->bsnj', xb, wa[d].astype(jnp.float32))
                           + ba[d].astype(jnp.float32)).reshape(B, S, LRU_WIDTH)
        i = jax.nn.sigmoid(jnp.einsum('bsni,nij->bsnj', xb, wx[d].astype(jnp.float32))
                           + bx[d].astype(jnp.float32)).reshape(B, S, LRU_WIDTH)
        log_a = -LRU_C * r * jax.nn.softplus(-lam[d].astype(jnp.float32))
        a = jnp.exp(log_a)
        b = jnp.sqrt(-jnp.expm1(2.0 * log_a)) * (i * xf)
        _, h = lax.associative_scan(_lin_combine, (a, b), axis=1, reverse=reverse)
        return h

    h = direction(0, False) + direction(1, True)
    return (h * jax.nn.gelu(gate_branch.astype(jnp.float32), approximate=False)).astype(x_branch.dtype)


def swa_mixer(q, k, v, positions, sink):
    B, S, _ = q.shape
    G = SWA_Q_HEADS // SWA_KV_HEADS
    nb = S // BLOCK_Q
    q = rope(q.reshape(B, S, SWA_Q_HEADS, SWA_HEAD_DIM), positions)
    q = q.reshape(B, nb, BLOCK_Q, SWA_KV_HEADS, G, SWA_HEAD_DIM)
    k = rope(k.reshape(B, S, SWA_KV_HEADS, SWA_HEAD_DIM), positions)
    v = v.reshape(B, S, SWA_KV_HEADS, SWA_HEAD_DIM)
    pad = ((0, 0), (SWA_WINDOW, SWA_WINDOW), (0, 0), (0, 0))

    def band(t):
        tp = jnp.pad(t, pad)
        views = [tp[:, j * BLOCK_Q: j * BLOCK_Q + S].reshape(B, nb, BLOCK_Q, SWA_KV_HEADS, SWA_HEAD_DIM)
                 for j in range(3)]
        return jnp.concatenate(views, axis=2)

    kb, vb = band(k), band(v)
    s = jnp.einsum('bnqkgd,bnjkd->bnkgqj', q, kb, preferred_element_type=jnp.float32) * SWA_HEAD_DIM ** -0.5
    qi = jnp.arange(BLOCK_Q)[None, :, None]
    kj = jnp.arange(3 * BLOCK_Q)[None, None, :]
    kglob = jnp.arange(nb)[:, None, None] * BLOCK_Q - SWA_WINDOW + kj
    valid = (jnp.abs(kj - SWA_WINDOW - qi) <= SWA_WINDOW) & (kglob >= 0) & (kglob < S)
    s = jnp.where(valid[None, :, None, None], s, NEG_BIG)
    sink_logit = jnp.broadcast_to(sink.astype(jnp.float32).reshape(1, 1, SWA_KV_HEADS, G, 1, 1),
                                  s.shape[:-1] + (1,))
    p = jax.nn.softmax(jnp.concatenate([s, sink_logit], axis=-1), axis=-1)[..., :-1]
    o = jnp.einsum('bnkgqj,bnjkd->bnqkgd', p.astype(vb.dtype), vb)
    return o.reshape(B, S, SWA_WIDTH)


def peer_ffn(x, wq, subkeys, u, v):
    B, S, D = x.shape
    T = B * S
    xt = x.reshape(T, D)
    q = (xt @ wq).reshape(T, PEER_HEADS, 2, PEER_DKEY // 2)
    s1 = jnp.einsum('thd,nd->thn', q[:, :, 0], subkeys[0], preferred_element_type=jnp.float32)
    s2 = jnp.einsum('thd,nd->thn', q[:, :, 1], subkeys[1], preferred_element_type=jnp.float32)
    v1, i1 = lax.top_k(s1, PEER_TOPK)
    v2, i2 = lax.top_k(s2, PEER_TOPK)
    cand = (v1[..., :, None] + v2[..., None, :]).reshape(T, PEER_HEADS, PEER_TOPK * PEER_TOPK)
    sc, ci = lax.top_k(cand, PEER_TOPK)
    e1 = jnp.take_along_axis(i1, ci // PEER_TOPK, axis=-1)
    e2 = jnp.take_along_axis(i2, ci % PEER_TOPK, axis=-1)
    nc = T // PEER_CHUNK
    expert = (e1 * PEER_NKEYS + e2).reshape(nc, PEER_CHUNK, PEER_HEADS * PEER_TOPK)
    gate = jax.nn.softmax(sc, axis=-1).astype(x.dtype).reshape(nc, PEER_CHUNK, PEER_HEADS * PEER_TOPK)

    def chunk(args):
        xc, ec, gc = args
        hid = jax.nn.gelu(jnp.einsum('cd,ced->ce', xc, u[ec]), approximate=False)
        return jnp.einsum('ce,ced->cd', gc * hid, v[ec])

    y = lax.map(chunk, (xt.reshape(nc, PEER_CHUNK, D), expert, gate))
    return y.reshape(B, S, D)


def setup_inputs(seed: int = 0) -> dict:
    key = jax.random.key(seed)
    ks = jax.random.split(key, 32)
    f32 = jnp.float32

    def nrm(k, shape, scale):
        return jax.random.normal(k, shape, f32) * scale

    def gain(k, shape):
        return 1.0 + 0.02 * jax.random.normal(k, shape, f32)

    x = nrm(ks[0], (BATCH, SEQ, D_MODEL), 1.0)
    offs = jax.random.randint(ks[1], (BATCH, 1), 0, 1024, dtype=jnp.int32)
    positions = offs + jnp.arange(SEQ, dtype=jnp.int32)[None, :]
    a_c = jax.random.uniform(ks[14], (DEPTH, 2, LRU_WIDTH), f32, minval=0.9, maxval=0.999)
    p_a = a_c ** (1.0 / LRU_C)
    lru_lambda = jnp.log(p_a) - jnp.log1p(-p_a)
    return {
        'x': x,
        'positions': positions,
        'norm_mix': gain(ks[2], (DEPTH, D_MODEL)),
        'w_in': nrm(ks[3], (DEPTH, D_MODEL, IN_WIDTH), D_MODEL ** -0.5),
        'q_norm': gain(ks[4], (DEPTH, MLA_Q_RANK)),
        'w_uq': nrm(ks[5], (DEPTH, MLA_Q_RANK, MLA_HEADS * (MLA_NOPE + MLA_ROPE)), MLA_Q_RANK ** -0.5),
        'kv_norm': gain(ks[6], (DEPTH, MLA_KV_RANK)),
        'w_ukv': nrm(ks[7], (DEPTH, MLA_KV_RANK, MLA_HEADS * (MLA_NOPE + MLA_V)), MLA_KV_RANK ** -0.5),
        'conv_w': nrm(ks[8], (DEPTH, CONV_WIDTH, LRU_WIDTH), CONV_WIDTH ** -0.5),
        'conv_b': nrm(ks[9], (DEPTH, LRU_WIDTH), 0.01),
        'lru_wa': nrm(ks[10], (DEPTH, 2, LRU_BLOCKS, LRU_BLOCK, LRU_BLOCK), LRU_BLOCK ** -0.5),
        'lru_ba': nrm(ks[11], (DEPTH, 2, LRU_BLOCKS, LRU_BLOCK), 0.01),
        'lru_wx': nrm(ks[12], (DEPTH, 2, LRU_BLOCKS, LRU_BLOCK, LRU_BLOCK), LRU_BLOCK ** -0.5),
        'lru_bx': nrm(ks[13], (DEPTH, 2, LRU_BLOCKS, LRU_BLOCK), 0.01),
        'lru_lambda': lru_lambda,
        'swa_sink': nrm(ks[15], (DEPTH, SWA_Q_HEADS), 1.0),
        'grp_norm': gain(ks[16], (DEPTH, MIX_WIDTH)),
        'w_out': nrm(ks[17], (DEPTH, MIX_WIDTH, D_MODEL), MIX_WIDTH ** -0.5),
        'norm_ffn': gain(ks[18], (DEPTH, D_MODEL)),
        'peer_wq': nrm(ks[19], (DEPTH, D_MODEL, PEER_HEADS * PEER_DKEY), D_MODEL ** -0.5),
        'peer_subkeys': nrm(ks[20], (DEPTH, 2, PEER_NKEYS, PEER_DKEY // 2), (PEER_DKEY // 2) ** -0.5),
        'peer_u': nrm(ks[21], (DEPTH, PEER_EXPERTS, D_MODEL), D_MODEL ** -0.5),
        'peer_v': nrm(ks[22], (DEPTH, PEER_EXPERTS, D_MODEL), 0.5),
        'norm_final': gain(ks[23], (D_MODEL,)),
    }


def reference(x, positions, norm_mix, w_in, q_norm, w_uq, kv_norm, w_ukv, conv_w, conv_b,
              lru_wa, lru_ba, lru_wx, lru_bx, lru_lambda, swa_sink, grp_norm, w_out,
              norm_ffn, peer_wq, peer_subkeys, peer_u, peer_v, norm_final):
    split_points = np.cumsum(IN_SPLITS)[:-1].tolist()
    g_a, g_b = MLA_WIDTH, MLA_WIDTH + LRU_WIDTH
    for l in range(DEPTH):
        xn = rmsnorm(x, norm_mix[l])
        proj = xn @ w_in[l]
        c_q, c_kv, k_r, s_q, s_k, s_v, l_x, l_g = jnp.split(proj, split_points, axis=-1)
        o_mla = mla_mixer(c_q, c_kv, k_r, positions, q_norm[l], w_uq[l], kv_norm[l], w_ukv[l])
        o_lru = rglru_mixer(l_x, l_g, conv_w[l], conv_b[l], lru_wa[l], lru_ba[l],
                            lru_wx[l], lru_bx[l], lru_lambda[l])
        o_swa = swa_mixer(s_q, s_k, s_v, positions, swa_sink[l])
        g = grp_norm[l]
        mixed = jnp.concatenate([rmsnorm(o_mla, g[:g_a]),
                                 rmsnorm(o_lru, g[g_a:g_b]),
                                 rmsnorm(o_swa, g[g_b:])], axis=-1)
        x = x + mixed @ w_out[l]
        x = x + peer_ffn(rmsnorm(x, norm_ffn[l]), peer_wq[l], peer_subkeys[l], peer_u[l], peer_v[l])
    return rmsnorm(x, norm_final)
```

```python
import functools

import jax
import jax.numpy as jnp
from jax import lax
from jax.experimental import pallas as pl
from jax.experimental.pallas import tpu as pltpu

F32 = jnp.float32
BF16 = jnp.bfloat16
I32 = jnp.int32

D_MODEL = 1024
MLA_HEADS, MLA_NOPE, MLA_ROPE, MLA_V = 6, 64, 32, 64
MLA_Q_RANK, MLA_KV_RANK = 256, 128
LRU_WIDTH, LRU_BLOCKS, LRU_C = 384, 6, 8.0
SWA_Q_HEADS, SWA_KV_HEADS, SWA_HEAD_DIM, SWA_WINDOW = 4, 2, 64, 128
ROPE_THETA, EPS, NEG_BIG = 10000.0, 1e-6, -1e30
PEER_HEADS, PEER_NKEYS, PEER_DKEY, PEER_TOPK = 8, 128, 256, 16
PEER_SLOTS = PEER_HEADS * PEER_TOPK
PEER_EXPERTS = PEER_NKEYS * PEER_NKEYS

LANES = 128
SUBLANES = 8
HP = LANES
MLA_QK = MLA_HEADS * HP
SWA_Q = SWA_Q_HEADS * HP
SWA_KV = SWA_KV_HEADS * HP
VMEM_LIMIT = 56 * 1024 * 1024
TABLE_HALVES = 2
HALF_EXPERTS = PEER_EXPERTS // TABLE_HALVES

_C_Q, _C_KV, _S_Q, _S_QR, _S_K, _S_KR, _S_V, _L_X, _L_G, _K_R, _IN_COLS = (
    0, 256, 384, 896, 1408, 1664, 1920, 2176, 2560, 2944, 3072)


def _cparams(*sem):
    return pltpu.CompilerParams(dimension_semantics=sem, vmem_limit_bytes=VMEM_LIMIT)


def _rmsnorm(x, g, n):
    return x * lax.rsqrt(jnp.sum(x * x, axis=-1, keepdims=True) * (1.0 / n) + EPS) * g


def _sigmoid(x):
    return 1.0 / (1.0 + jnp.exp(-x))


def _gelu(x):
    return 0.5 * x * (1.0 + lax.erf(x * (2.0 ** -0.5)))


def _rot_half_cols(w, d):
    k = w.shape[0]
    w3 = w.reshape(k, -1, d)
    return jnp.concatenate([-w3[..., d // 2:], w3[..., : d // 2]], axis=-1).reshape(k, -1)


def _pad_heads(w, d, lead=0):
    k = w.shape[0]
    w3 = w.reshape(k, -1, d)
    return jnp.pad(w3, ((0, 0), (0, 0), (lead, HP - d - lead))).reshape(k, -1)


def _inv_freq(d):
    return ROPE_THETA ** (-jnp.arange(0, d, 2, dtype=F32) / d)


def _layer_weights(l, w_in, w_uq, w_ukv, lru_wa, lru_ba, lru_wx, lru_bx, grp_norm, w_out):
    wi = w_in[l]
    c_q, c_kv, k_r = wi[:, 0:256], wi[:, 256:384], wi[:, 384:416]
    s_q, s_k, s_v = wi[:, 416:672], wi[:, 672:800], wi[:, 800:928]
    l_x, l_g = wi[:, 928:1312], wi[:, 1312:1696]
    k_r_blk = jnp.concatenate([jnp.zeros((D_MODEL, MLA_NOPE), F32), k_r, _rot_half_cols(k_r, MLA_ROPE)], axis=1)
    w_in_p = jnp.concatenate([
        c_q, c_kv,
        _pad_heads(s_q, SWA_HEAD_DIM), _pad_heads(_rot_half_cols(s_q, SWA_HEAD_DIM), SWA_HEAD_DIM),
        _pad_heads(s_k, SWA_HEAD_DIM), _pad_heads(_rot_half_cols(s_k, SWA_HEAD_DIM), SWA_HEAD_DIM),
        _pad_heads(s_v, SWA_HEAD_DIM), l_x, l_g, k_r_blk], axis=1).astype(BF16)

    wq3 = w_uq[l].reshape(MLA_Q_RANK, MLA_HEADS, MLA_NOPE + MLA_ROPE)
    q_nope = wq3[..., :MLA_NOPE].reshape(MLA_Q_RANK, -1)
    q_rope = wq3[..., MLA_NOPE:].reshape(MLA_Q_RANK, -1)
    wq = jnp.concatenate([
        _pad_heads(q_nope, MLA_NOPE) + _pad_heads(q_rope, MLA_ROPE, lead=MLA_NOPE),
        _pad_heads(_rot_half_cols(q_rope, MLA_ROPE), MLA_ROPE, lead=MLA_NOPE)], axis=1).astype(BF16)

    wkv3 = w_ukv[l].reshape(MLA_KV_RANK, MLA_HEADS, MLA_NOPE + MLA_V)
    wkv = jnp.concatenate([
        _pad_heads(wkv3[..., :MLA_NOPE].reshape(MLA_KV_RANK, -1), MLA_NOPE),
        _pad_heads(wkv3[..., MLA_NOPE:].reshape(MLA_KV_RANK, -1), MLA_V)], axis=1).astype(BF16)

    def blockdiag(w):
        n, bi, bj = w.shape
        eye = jnp.eye(n, dtype=w.dtype)
        return (w[:, :, None, :] * eye[:, None, :, None]).reshape(n * bi, n * bj)

    w_gate = [jnp.concatenate([blockdiag(lru_wa[l, d]), blockdiag(lru_wx[l, d])], axis=1).astype(BF16) for d in range(2)]
    b_gate = [jnp.concatenate([lru_ba[l, d].reshape(1, -1), lru_bx[l, d].reshape(1, -1)], axis=1) for d in range(2)]

    g = grp_norm[l]
    wo = w_out[l]
    ga, gb = MLA_HEADS * MLA_V, MLA_HEADS * MLA_V + LRU_WIDTH

    def pad_rows(w, d):
        return jnp.pad(w.reshape(-1, d, w.shape[-1]), ((0, 0), (0, HP - d), (0, 0))).reshape(-1, w.shape[-1])

    out_w = dict(
        g_a=_pad_heads(g[None, :ga], MLA_V), g_b=g[None, ga:gb], g_c=_pad_heads(g[None, gb:], SWA_HEAD_DIM),
        w_a=pad_rows(wo[:ga], MLA_V).astype(BF16), w_b=wo[ga:gb].astype(BF16),
        w_c=pad_rows(wo[gb:], SWA_HEAD_DIM).astype(BF16))
    return w_in_p, wq, wkv, w_gate, b_gate, out_w


def _rope_table_kernel(pos_ref, inv_ref, cos_ref, sin_ref):
    ang = pos_ref[...] * inv_ref[...]
    cos_ref[...] = jnp.cos(ang)
    sin_ref[...] = jnp.sin(ang)


def _rope_tables(pos, inv, tm=512):
    t, w = pos.shape[0], inv.shape[1]
    return pl.pallas_call(
        _rope_table_kernel, name="rope_tables",
        grid=(t // tm,),
        in_specs=[pl.BlockSpec((tm, 1), lambda i: (i, 0)), pl.BlockSpec((1, w), lambda i: (0, 0))],
        out_specs=[pl.BlockSpec((tm, w), lambda i: (i, 0))] * 2,
        out_shape=[jax.ShapeDtypeStruct((t, w), F32)] * 2,
        compiler_params=_cparams("parallel"),
    )(pos, inv)


def _in_proj_kernel(x_ref, nmix_ref, win_ref, qn_ref, wq_ref, kvn_ref, wkv_ref, cosq_ref, sinq_ref, coss_ref, sins_ref,
                    q_out, k_out, v_out, sq_out, sk_out, sv_out, lx_out, lg_out):
    xn = _rmsnorm(x_ref[...], nmix_ref[...], D_MODEL)
    proj = jnp.dot(xn.astype(BF16), win_ref[...], preferred_element_type=F32)
    cosq, sinq, coss, sins = cosq_ref[...], sinq_ref[...], coss_ref[...], sins_ref[...]

    swa_scale = SWA_HEAD_DIM ** -0.5
    sq_out[...] = ((proj[:, _S_Q:_S_QR] * coss + proj[:, _S_QR:_S_K] * sins) * swa_scale).astype(BF16)
    sk_out[...] = (proj[:, _S_K:_S_KR] * coss[:, :SWA_KV] + proj[:, _S_KR:_S_V] * sins[:, :SWA_KV]).astype(BF16)
    sv_out[...] = proj[:, _S_V:_L_X].astype(BF16)
    lx_out[...] = proj[:, _L_X:_L_G]
    lg_out[...] = proj[:, _L_G:_K_R]

    cqn = _rmsnorm(proj[:, _C_Q:_C_KV], qn_ref[...], MLA_Q_RANK).astype(BF16)
    qq = jnp.dot(cqn, wq_ref[...], preferred_element_type=F32)
    mla_scale = (MLA_NOPE + MLA_ROPE) ** -0.5
    q_out[...] = ((qq[:, :MLA_QK] * cosq + qq[:, MLA_QK:] * sinq) * mla_scale).astype(BF16)

    kvn = _rmsnorm(proj[:, _C_KV:_S_Q], kvn_ref[...], MLA_KV_RANK).astype(BF16)
    kv = jnp.dot(kvn, wkv_ref[...], preferred_element_type=F32)
    kr = proj[:, _K_R:_IN_COLS]
    lane = lax.broadcasted_iota(I32, kr.shape, 1)
    in_rope = (lane >= MLA_NOPE) & (lane < MLA_NOPE + MLA_ROPE)
    kr_rot = jnp.where(in_rope, kr * cosq[:, :HP] + pltpu.roll(kr, HP - MLA_ROPE, axis=1) * sinq[:, :HP], 0.0)
    for h in range(MLA_HEADS):
        k_out[:, h * HP:(h + 1) * HP] = (kv[:, h * HP:(h + 1) * HP] + kr_rot).astype(BF16)
    v_out[...] = kv[:, MLA_QK:].astype(BF16)


def _in_proj(x, nmix, w_in_p, qn, wq, kvn, wkv, cosq, sinq, coss, sins, tm=256):
    t = x.shape[0]
    row = lambda w: pl.BlockSpec((tm, w), lambda i: (i, 0))
    full = lambda a: pl.BlockSpec(a.shape, lambda i: (0,) * a.ndim)
    widths = (MLA_QK, MLA_QK, MLA_QK, SWA_Q, SWA_KV, SWA_KV, LRU_WIDTH, LRU_WIDTH)
    dtypes = (BF16, BF16, BF16, BF16, BF16, BF16, F32, F32)
    return pl.pallas_call(
        _in_proj_kernel, name="in_proj",
        grid=(t // tm,),
        in_specs=[row(D_MODEL), full(nmix), full(w_in_p), full(qn), full(wq), full(kvn), full(wkv),
                  row(MLA_QK), row(MLA_QK), row(SWA_Q), row(SWA_Q)],
        out_specs=[row(w) for w in widths],
        out_shape=[jax.ShapeDtypeStruct((t, w), d) for w, d in zip(widths, dtypes)],
        compiler_params=_cparams("parallel"),
    )(x, nmix, w_in_p, qn, wq, kvn, wkv, cosq, sinq, coss, sins)


def _mla_kernel(q_ref, k_ref, v_ref, o_ref, *, tk):
    q = q_ref[...]
    tq = q.shape[0]
    nk = k_ref.shape[0] // tk

    def body(j, carry):
        m, l, acc = carry
        off = pl.multiple_of(j * tk, tk)
        k = k_ref[pl.ds(off, tk), :]
        v = v_ref[pl.ds(off, tk), :]
        s = lax.dot_general(q, k, (((1,), (1,)), ((), ())), preferred_element_type=F32)
        m_new = jnp.maximum(m, jnp.max(s, axis=-1, keepdims=True))
        alpha = jnp.exp(m - m_new)
        p = jnp.exp(s - m_new)
        l = alpha * l + jnp.sum(p, axis=-1, keepdims=True)
        acc = alpha * acc + jnp.dot(p.astype(BF16), v, preferred_element_type=F32)
        return m_new, l, acc

    init = (jnp.full((tq, 1), -jnp.inf, F32), jnp.zeros((tq, 1), F32), jnp.zeros((tq, HP), F32))
    _, l, acc = lax.fori_loop(0, nk, body, init)
    o_ref[...] = acc / l


def _mla_attention(q, k, v, batch, seq, tq=512, tk=512):
    nq = seq // tq
    return pl.pallas_call(
        functools.partial(_mla_kernel, tk=tk), name="mla_attention",
        grid=(batch, MLA_HEADS, nq),
        in_specs=[pl.BlockSpec((tq, HP), lambda b, h, i: (b * nq + i, h)),
                  pl.BlockSpec((seq, HP), lambda b, h, i: (b, h)),
                  pl.BlockSpec((seq, HP), lambda b, h, i: (b, h))],
        out_specs=pl.BlockSpec((tq, HP), lambda b, h, i: (b * nq + i, h)),
        out_shape=jax.ShapeDtypeStruct((batch * seq, MLA_QK), F32),
        compiler_params=_cparams("parallel", "parallel", "arbitrary"),
    )(q, k, v)


def _swa_kernel(sink_ref, q_ref, kp_ref, kc_ref, kn_ref, vp_ref, vc_ref, vn_ref, o_ref, *, seq):
    w = SWA_WINDOW
    tq = q_ref.shape[0]
    i = pl.program_id(1)
    kcat = jnp.concatenate([kp_ref[...], kc_ref[...], kn_ref[...]], axis=0)
    vcat = jnp.concatenate([vp_ref[...], vc_ref[...], vn_ref[...]], axis=0)
    for j in range(tq // w):
        qpos = i * tq + j * w + lax.broadcasted_iota(I32, (w, 3 * w), 0)
        kpos = i * tq + (j - 1) * w + lax.broadcasted_iota(I32, (w, 3 * w), 1)
        valid = (jnp.abs(kpos - qpos) <= w) & (kpos >= 0) & (kpos < seq)
        for h in range(SWA_Q_HEADS):
            kh = h // (SWA_Q_HEADS // SWA_KV_HEADS)
            qh = q_ref[j * w:(j + 1) * w, h * HP:(h + 1) * HP]
            kj = kcat[j * w:(j + 3) * w, kh * HP:(kh + 1) * HP]
            vj = vcat[j * w:(j + 3) * w, kh * HP:(kh + 1) * HP]
            s = lax.dot_general(qh, kj, (((1,), (1,)), ((), ())), preferred_element_type=F32)
            s = jnp.where(valid, s, NEG_BIG)
            sink = sink_ref[0, h]
            m = jnp.maximum(jnp.max(s, axis=-1, keepdims=True), sink)
            p = jnp.exp(s - m)
            den = jnp.sum(p, axis=-1, keepdims=True) + jnp.exp(sink - m)
            o_ref[j * w:(j + 1) * w, h * HP:(h + 1) * HP] = jnp.dot(p.astype(BF16), vj, preferred_element_type=F32) / den


def _swa_attention(sq, sk, sv, sink, batch, seq, tq=512):
    w = SWA_WINDOW
    nq, nw, r = seq // tq, seq // w, tq // w
    prev = pl.BlockSpec((w, SWA_KV), lambda b, i: (b * nw + jnp.maximum(i * r - 1, 0), 0))
    cur = pl.BlockSpec((tq, SWA_KV), lambda b, i: (b * nq + i, 0))
    nxt = pl.BlockSpec((w, SWA_KV), lambda b, i: (b * nw + jnp.minimum((i + 1) * r, nw - 1), 0))
    return pl.pallas_call(
        functools.partial(_swa_kernel, seq=seq), name="swa_attention",
        grid=(batch, nq),
        in_specs=[pl.BlockSpec(memory_space=pltpu.SMEM),
                  pl.BlockSpec((tq, SWA_Q), lambda b, i: (b * nq + i, 0)),
                  prev, cur, nxt, prev, cur, nxt],
        out_specs=pl.BlockSpec((tq, SWA_Q), lambda b, i: (b * nq + i, 0)),
        out_shape=jax.ShapeDtypeStruct((batch * seq, SWA_Q), F32),
        compiler_params=_cparams("parallel", "parallel"),
    )(sink, sq, sk, sk, sk, sv, sv, sv)


def _lru_kernel(xpf_ref, xcf_ref, xnf_ref, xpb_ref, xcb_ref, xnb_ref, cw_ref, cb_ref, wf_ref, bf_ref, wb_ref, bb_ref,
                lam_ref, hf_out, hb_out, carry_f, carry_b, *, nt):
    tm = xcf_ref.shape[0]
    i = pl.program_id(1)
    halo = xpf_ref.shape[0]

    @pl.when(i == 0)
    def _():
        carry_f[...] = jnp.zeros_like(carry_f)
        carry_b[...] = jnp.zeros_like(carry_b)

    def gates(xp_ref, xc_ref, xn_ref, first, last, w_ref, b_ref, lam):
        prev = jnp.where(first, 0.0, xp_ref[...])
        nxt = jnp.where(last, 0.0, xn_ref[...])
        xcat = jnp.concatenate([prev, xc_ref[...], nxt], axis=0)
        cw = cw_ref[...]
        conv = cb_ref[...]
        for tap in range(cw.shape[0]):
            conv = conv + cw[tap:tap + 1, :] * xcat[halo - 1 + tap: halo - 1 + tap + tm, :]
        g = jnp.dot(conv.astype(BF16), w_ref[...], preferred_element_type=F32) + b_ref[...]
        r = _sigmoid(g[:, :LRU_WIDTH])
        gate_i = _sigmoid(g[:, LRU_WIDTH:])
        softplus = jnp.maximum(-lam, 0.0) + jnp.log1p(jnp.exp(-jnp.abs(lam)))
        log_a = -LRU_C * r * softplus
        a = jnp.exp(log_a)
        b = jnp.sqrt(1.0 - a * a) * (gate_i * conv)
        return a, b

    row = lax.broadcasted_iota(I32, (tm, LRU_WIDTH), 0)

    def scan(a, b, reverse):
        k = 1
        while k < tm:
            if reverse:
                keep = row < tm - k
                shift = tm - k
            else:
                keep = row >= k
                shift = k
            a_s = jnp.where(keep, pltpu.roll(a, shift, axis=0), 1.0)
            b_s = jnp.where(keep, pltpu.roll(b, shift, axis=0), 0.0)
            b = a * b_s + b
            a = a * a_s
            k *= 2
        return a, b

    a, b = gates(xpf_ref, xcf_ref, xnf_ref, i == 0, i == nt - 1, wf_ref, bf_ref, lam_ref[0:1, :])
    a, b = scan(a, b, False)
    h = a * carry_f[...] + b
    hf_out[...] = h
    carry_f[...] = h[tm - 1:tm, :]

    a, b = gates(xpb_ref, xcb_ref, xnb_ref, i == nt - 1, i == 0, wb_ref, bb_ref, lam_ref[1:2, :])
    a, b = scan(a, b, True)
    h = a * carry_b[...] + b
    hb_out[...] = h
    carry_b[...] = h[0:1, :]


def _lru_scan(lx, conv_w, conv_b, w_gate, b_gate, lam, batch, seq, tm=256):
    nt, hb = seq // tm, seq // SUBLANES
    r = tm // SUBLANES
    fwd = lambda b, i: i
    bwd = lambda b, i: nt - 1 - i

    def specs(tile):
        return [pl.BlockSpec((SUBLANES, LRU_WIDTH), lambda b, i: (b * hb + jnp.maximum(tile(b, i) * r - 1, 0), 0)),
                pl.BlockSpec((tm, LRU_WIDTH), lambda b, i: (b * nt + tile(b, i), 0)),
                pl.BlockSpec((SUBLANES, LRU_WIDTH), lambda b, i: (b * hb + jnp.minimum((tile(b, i) + 1) * r, hb - 1), 0))]

    full = lambda a: pl.BlockSpec(a.shape, lambda b, i: (0,) * a.ndim)
    return pl.pallas_call(
        functools.partial(_lru_kernel, nt=nt), name="lru_scan",
        grid=(batch, nt),
        in_specs=specs(fwd) + specs(bwd) + [full(conv_w), full(conv_b), full(w_gate[0]), full(b_gate[0]),
                                            full(w_gate[1]), full(b_gate[1]), full(lam)],
        out_specs=[pl.BlockSpec((tm, LRU_WIDTH), lambda b, i: (b * nt + i, 0)),
                   pl.BlockSpec((tm, LRU_WIDTH), lambda b, i: (b * nt + nt - 1 - i, 0))],
        out_shape=[jax.ShapeDtypeStruct((batch * seq, LRU_WIDTH), F32)] * 2,
        scratch_shapes=[pltpu.VMEM((1, LRU_WIDTH), F32), pltpu.VMEM((1, LRU_WIDTH), F32)],
        compiler_params=_cparams("parallel", "arbitrary"),
    )(lx, lx, lx, lx, lx, lx, conv_w, conv_b, w_gate[0], b_gate[0], w_gate[1], b_gate[1], lam)


def _out_proj_kernel(x_ref, oa_ref, hf_ref, hb_ref, lg_ref, oc_ref, ga_ref, gb_ref, gc_ref, wa_ref, wb_ref, wc_ref,
                     nffn_ref, wq_ref, x1_out, xn_out, q_out):
    mix_a = _rmsnorm(oa_ref[...], ga_ref[...], MLA_HEADS * MLA_V).astype(BF16)
    o_lru = (hf_ref[...] + hb_ref[...]) * _gelu(lg_ref[...])
    mix_b = _rmsnorm(o_lru, gb_ref[...], LRU_WIDTH).astype(BF16)
    mix_c = _rmsnorm(oc_ref[...], gc_ref[...], SWA_Q_HEADS * SWA_HEAD_DIM).astype(BF16)
    x1 = (x_ref[...]
          + jnp.dot(mix_a, wa_ref[...], preferred_element_type=F32)
          + jnp.dot(mix_b, wb_ref[...], preferred_element_type=F32)
          + jnp.dot(mix_c, wc_ref[...], preferred_element_type=F32))
    x1_out[...] = x1
    xn = _rmsnorm(x1, nffn_ref[...], D_MODEL)
    xn_out[...] = xn
    q_out[...] = jnp.dot(xn.astype(BF16), wq_ref[...], preferred_element_type=F32)


def _out_proj(x, o_mla, h_f, h_b, l_g, o_swa, ow, nffn, wq, tm=256):
    t = x.shape[0]
    row = lambda w: pl.BlockSpec((tm, w), lambda i: (i, 0))
    full = lambda a: pl.BlockSpec(a.shape, lambda i: (0,) * a.ndim)
    nq = wq.shape[1]
    return pl.pallas_call(
        _out_proj_kernel, name="out_proj",
        grid=(t // tm,),
        in_specs=[row(D_MODEL), row(MLA_QK), row(LRU_WIDTH), row(LRU_WIDTH), row(LRU_WIDTH), row(SWA_Q),
                  full(ow["g_a"]), full(ow["g_b"]), full(ow["g_c"]), full(ow["w_a"]), full(ow["w_b"]), full(ow["w_c"]),
                  full(nffn), full(wq)],
        out_specs=[row(D_MODEL), row(D_MODEL), row(nq)],
        out_shape=[jax.ShapeDtypeStruct((t, D_MODEL), F32), jax.ShapeDtypeStruct((t, D_MODEL), F32),
                   jax.ShapeDtypeStruct((t, nq), F32)],
        compiler_params=_cparams("parallel"),
    )(x, o_mla, h_f, h_b, l_g, o_swa, ow["g_a"], ow["g_b"], ow["g_c"], ow["w_a"], ow["w_b"], ow["w_c"], nffn, wq)


def _top_k_rows(s, k, payload=None):
    n = s.shape[0]
    row = lax.broadcasted_iota(I32, s.shape, 0)
    vals, picks = [], []
    for _ in range(k):
        m = jnp.max(s, axis=0, keepdims=True)
        idx = jnp.min(jnp.where(s == m, row, n), axis=0, keepdims=True)
        hit = row == idx
        vals.append(m)
        picks.append(idx if payload is None else jnp.sum(jnp.where(hit, payload, 0), axis=0, keepdims=True))
        s = jnp.where(hit, -jnp.inf, s)
    return jnp.concatenate(vals, axis=0), jnp.concatenate(picks, axis=0)


def _peer_topk_kernel(q_ref, keys_ref, id_out, gate_out):
    half = PEER_DKEY // 2
    dn = (((1,), (1,)), ((), ()))
    q = q_ref[...]
    s1 = lax.dot_general(keys_ref[0], q[:, :half], dn, preferred_element_type=F32, precision=lax.Precision.HIGHEST)
    s2 = lax.dot_general(keys_ref[1], q[:, half:], dn, preferred_element_type=F32, precision=lax.Precision.HIGHEST)
    v1, i1 = _top_k_rows(s1, PEER_TOPK)
    v2, i2 = _top_k_rows(s2, PEER_TOPK)
    cand = jnp.concatenate([v1[a:a + 1, :] + v2 for a in range(PEER_TOPK)], axis=0)
    cand_id = jnp.concatenate([i1[a:a + 1, :] * PEER_NKEYS + i2 for a in range(PEER_TOPK)], axis=0)
    sc, ids = _top_k_rows(cand, PEER_TOPK, payload=cand_id)
    e = jnp.exp(sc - sc[0:1, :])
    id_out[0] = ids
    gate_out[0] = e / jnp.sum(e, axis=0, keepdims=True)


def _peer_topk(q, subkeys, tm=256):
    t = q.shape[0]
    out = pl.BlockSpec((1, PEER_TOPK, tm), lambda i, h: (h, 0, i))
    return pl.pallas_call(
        _peer_topk_kernel, name="peer_topk",
        grid=(t // tm, PEER_HEADS),
        in_specs=[pl.BlockSpec((tm, PEER_DKEY), lambda i, h: (i, h)),
                  pl.BlockSpec(subkeys.shape, lambda i, h: (0, 0, 0))],
        out_specs=[out, out],
        out_shape=[jax.ShapeDtypeStruct((PEER_HEADS, PEER_TOPK, t), I32),
                   jax.ShapeDtypeStruct((PEER_HEADS, PEER_TOPK, t), F32)],
        compiler_params=_cparams("parallel", "parallel"),
    )(q, subkeys)


def _fold(vals, axis, masks):
    for level, mask in enumerate(masks):
        shift = 1 << level
        nxt = []
        for a, b in zip(vals[0::2], vals[1::2]):
            nxt.append(jnp.where(mask, a, b) + pltpu.roll(jnp.where(mask, b, a), shift, axis=axis))
        vals = nxt
    (out,) = vals
    return out


def _peer_hidden_kernel(id_ref, x_ref, u_ref, h_out):
    tm = x_ref.shape[0]
    sub = lax.broadcasted_iota(I32, (SUBLANES, LANES), 0)
    lane = lax.broadcasted_iota(I32, (SUBLANES, LANES), 1)
    sub_masks = [(sub % (2 << lv)) < (1 << lv) for lv in range(3)]
    lane_masks = [(lane % (2 << lv)) < (1 << lv) for lv in range(4)]
    groups = PEER_SLOTS // SUBLANES

    def token(tt, carry):
        g, out = carry
        t = g * SUBLANES + tt
        xt = x_ref[t]
        rows = []
        for j in range(groups):
            prods = [u_ref[id_ref[t, j * SUBLANES + s]] * xt for s in range(SUBLANES)]
            rows.append(_fold(prods, 0, sub_masks))
        h = _fold(rows, 1, lane_masks)
        for shift in (16, 32, 64):
            h = h + pltpu.roll(h, shift, axis=1)
        return g, jnp.where(lane // groups == tt, h, out)

    def group(g, _):
        _, out = lax.fori_loop(0, SUBLANES, token, (g, jnp.zeros((SUBLANES, LANES), F32)))
        h_out[g] = out
        return 0

    lax.fori_loop(0, tm // SUBLANES, group, 0)


def _peer_hidden(lid, x3, u3, tm=128):
    t = x3.shape[0]
    return pl.pallas_call(
        _peer_hidden_kernel, name="peer_hidden",
        grid=(TABLE_HALVES, t // tm),
        in_specs=[pl.BlockSpec((None, tm, PEER_SLOTS), lambda p, i: (p, i, 0), memory_space=pltpu.SMEM),
                  pl.BlockSpec((tm, SUBLANES, LANES), lambda p, i: (i, 0, 0)),
                  pl.BlockSpec((HALF_EXPERTS, SUBLANES, LANES), lambda p, i: (p, 0, 0), pipeline_mode=pl.Buffered(1))],
        out_specs=pl.BlockSpec((None, tm // SUBLANES, SUBLANES, LANES), lambda p, i: (p, i, 0, 0)),
        out_shape=jax.ShapeDtypeStruct((TABLE_HALVES, t // SUBLANES, SUBLANES, LANES), F32),
        compiler_params=_cparams("arbitrary", "arbitrary"),
    )(lid, x3, u3)


def _peer_coef_kernel(h_ref, gate_ref, id_ref, c_out):
    ids = id_ref[...]
    low = ids < HALF_EXPERTS
    c = gate_ref[...] * _gelu(jnp.where(low, h_ref[0], h_ref[1]))
    c_out[0] = jnp.where(low, c, 0.0)
    c_out[1] = jnp.where(low, 0.0, c)


def _peer_coef(hid, gate, ids, tm=1024):
    t = gate.shape[0]
    return pl.pallas_call(
        _peer_coef_kernel, name="peer_coef",
        grid=(t // tm,),
        in_specs=[pl.BlockSpec((TABLE_HALVES, tm, PEER_SLOTS), lambda i: (0, i, 0)),
                  pl.BlockSpec((tm, PEER_SLOTS), lambda i: (i, 0)),
                  pl.BlockSpec((tm, PEER_SLOTS), lambda i: (i, 0))],
        out_specs=pl.BlockSpec((TABLE_HALVES, tm, PEER_SLOTS), lambda i: (0, i, 0)),
        out_shape=jax.ShapeDtypeStruct((TABLE_HALVES, t, PEER_SLOTS), F32),
        compiler_params=_cparams("parallel"),
    )(hid, gate, ids)


def _peer_value_kernel(id_ref, c_ref, v_ref, y_out):
    tm = y_out.shape[0]
    n_acc = 4

    def token(t, _):
        acc = [jnp.zeros((SUBLANES, LANES), F32) for _ in range(n_acc)]
        for e in range(PEER_SLOTS):
            acc[e % n_acc] = acc[e % n_acc] + c_ref[t, e] * v_ref[id_ref[t, e]]
        y_out[t] = (acc[0] + acc[1]) + (acc[2] + acc[3])
        return 0

    lax.fori_loop(0, tm, token, 0)


def _peer_value(lid, coef, v3, tm=128):
    t = lid.shape[1]
    smem = pl.BlockSpec((None, tm, PEER_SLOTS), lambda p, i: (p, i, 0), memory_space=pltpu.SMEM)
    return pl.pallas_call(
        _peer_value_kernel, name="peer_value",
        grid=(TABLE_HALVES, t // tm),
        in_specs=[smem, smem,
                  pl.BlockSpec((HALF_EXPERTS, SUBLANES, LANES), lambda p, i: (p, 0, 0), pipeline_mode=pl.Buffered(1))],
        out_specs=pl.BlockSpec((None, tm, SUBLANES, LANES), lambda p, i: (p, i, 0, 0)),
        out_shape=jax.ShapeDtypeStruct((TABLE_HALVES, t, SUBLANES, LANES), F32),
        compiler_params=_cparams("arbitrary", "arbitrary"),
    )(lid, coef, v3)


def _residual_kernel(x_ref, y_ref, g_ref, o_ref, *, final):
    x = x_ref[...] + y_ref[0] + y_ref[1]
    o_ref[...] = _rmsnorm(x, g_ref[...], D_MODEL) if final else x


def _residual(x1, y, gain, final, tm=512):
    t = x1.shape[0]
    return pl.pallas_call(
        functools.partial(_residual_kernel, final=final), name="peer_residual",
        grid=(t // tm,),
        in_specs=[pl.BlockSpec((tm, D_MODEL), lambda i: (i, 0)),
                  pl.BlockSpec((TABLE_HALVES, tm, D_MODEL), lambda i: (0, i, 0)),
                  pl.BlockSpec((1, D_MODEL), lambda i: (0, 0))],
        out_specs=pl.BlockSpec((tm, D_MODEL), lambda i: (i, 0)),
        out_shape=jax.ShapeDtypeStruct((t, D_MODEL), F32),
        compiler_params=_cparams("parallel"),
    )(x1, y, gain)


def _peer_ffn(x1, xn, q, subkeys, u, v, gain, final):
    t = x1.shape[0]
    ids_h, gate_h = _peer_topk(q, subkeys)
    ids = ids_h.reshape(PEER_SLOTS, t).T
    gate = gate_h.reshape(PEER_SLOTS, t).T
    base = (jnp.arange(TABLE_HALVES, dtype=I32) * HALF_EXPERTS)[:, None, None]
    lid = jnp.clip(ids[None] - base, 0, HALF_EXPERTS - 1)
    h_raw = _peer_hidden(lid, xn.reshape(t, SUBLANES, LANES), u.reshape(PEER_EXPERTS, SUBLANES, LANES))
    hid = h_raw.reshape(TABLE_HALVES, t // SUBLANES, SUBLANES, SUBLANES, PEER_SLOTS // SUBLANES)
    hid = hid.transpose(0, 1, 3, 4, 2).reshape(TABLE_HALVES, t, PEER_SLOTS)
    coef = _peer_coef(hid, gate, ids)
    y = _peer_value(lid, coef, v.reshape(PEER_EXPERTS, SUBLANES, LANES))
    return _residual(x1, y.reshape(TABLE_HALVES, t, D_MODEL), gain, final)


def kernel(x, positions, norm_mix, w_in, q_norm, w_uq, kv_norm, w_ukv, conv_w, conv_b, lru_wa, lru_ba, lru_wx, lru_bx,
           lru_lambda, swa_sink, grp_norm, w_out, norm_ffn, peer_wq, peer_subkeys, peer_u, peer_v, norm_final):
    batch, seq, _ = x.shape
    t = batch * seq
    depth = w_in.shape[0]
    xt = x.reshape(t, D_MODEL)

    pos = positions.astype(F32).reshape(t, 1)
    zeros = lambda n: jnp.zeros((n,), F32)
    inv_m, inv_s = _inv_freq(MLA_ROPE), _inv_freq(SWA_HEAD_DIM)
    inv_q = jnp.tile(jnp.concatenate([zeros(MLA_NOPE), inv_m, inv_m, zeros(HP - MLA_NOPE - MLA_ROPE)]), MLA_HEADS)
    inv_w = jnp.tile(jnp.concatenate([inv_s, inv_s, zeros(HP - SWA_HEAD_DIM)]), SWA_Q_HEADS)
    cosq, sinq = _rope_tables(pos, inv_q[None, :])
    coss, sins = _rope_tables(pos, inv_w[None, :])

    for l in range(depth):
        w_in_p, wq, wkv, w_gate, b_gate, ow = _layer_weights(
            l, w_in, w_uq, w_ukv, lru_wa, lru_ba, lru_wx, lru_bx, grp_norm, w_out)
        q, k, v, sq, sk, sv, lx, lg = _in_proj(
            xt, norm_mix[l][None, :], w_in_p, q_norm[l][None, :], wq, kv_norm[l][None, :], wkv, cosq, sinq, coss, sins)
        o_mla = _mla_attention(q, k, v, batch, seq)
        o_swa = _swa_attention(sq, sk, sv, swa_sink[l][None, :], batch, seq)
        h_f, h_b = _lru_scan(lx, conv_w[l], conv_b[l][None, :], w_gate, b_gate, lru_lambda[l], batch, seq)
        x1, xn, pq = _out_proj(xt, o_mla, h_f, h_b, lg, o_swa, ow, norm_ffn[l][None, :], peer_wq[l].astype(BF16))
        final = l == depth - 1
        gain = norm_final[None, :] if final else jnp.ones((1, D_MODEL), F32)
        xt = _peer_ffn(x1, xn, pq, peer_subkeys[l], peer_u[l], peer_v[l], gain, final)
    return xt.reshape(batch, seq, D_MODEL)
```

```python
import functools

import jax
import jax.numpy as jnp
from jax import lax
from jax.experimental import pallas as pl
from jax.experimental.pallas import tpu as pltpu

F32 = jnp.float32
BF16 = jnp.bfloat16
I32 = jnp.int32

D_MODEL = 1024
MLA_HEADS, MLA_NOPE, MLA_ROPE, MLA_V = 6, 64, 32, 64
MLA_Q_RANK, MLA_KV_RANK = 256, 128
LRU_WIDTH, LRU_BLOCKS, LRU_C = 384, 6, 8.0
SWA_Q_HEADS, SWA_KV_HEADS, SWA_HEAD_DIM, SWA_WINDOW = 4, 2, 64, 128
ROPE_THETA, EPS, NEG_BIG = 10000.0, 1e-6, -1e30
PEER_HEADS, PEER_NKEYS, PEER_DKEY, PEER_TOPK = 8, 128, 256, 16
PEER_SLOTS = PEER_HEADS * PEER_TOPK
PEER_EXPERTS = PEER_NKEYS * PEER_NKEYS

LANES = 128
SUBLANES = 8
HP = LANES
MLA_QK = MLA_HEADS * HP
SWA_Q = SWA_Q_HEADS * HP
SWA_KV = SWA_KV_HEADS * HP
VMEM_LIMIT = 56 * 1024 * 1024
TABLE_HALVES = 2
HALF_EXPERTS = PEER_EXPERTS // TABLE_HALVES

_C_Q, _C_KV, _S_Q, _S_QR, _S_K, _S_KR, _S_V, _L_X, _L_G, _K_R, _IN_COLS = (
    0, 256, 384, 896, 1408, 1664, 1920, 2176, 2560, 2944, 3072)


def _cparams(*sem):
    return pltpu.CompilerParams(dimension_semantics=sem, vmem_limit_bytes=VMEM_LIMIT)


def _rmsnorm(x, g, n):
    return x * lax.rsqrt(jnp.sum(x * x, axis=-1, keepdims=True) * (1.0 / n) + EPS) * g


def _sigmoid(x):
    return 1.0 / (1.0 + jnp.exp(-x))


def _gelu(x):
    return 0.5 * x * (1.0 + lax.erf(x * (2.0 ** -0.5)))


def _rot_half_cols(w, d):
    k = w.shape[0]
    w3 = w.reshape(k, -1, d)
    return jnp.concatenate([-w3[..., d // 2:], w3[..., : d // 2]], axis=-1).reshape(k, -1)


def _pad_heads(w, d, lead=0):
    k = w.shape[0]
    w3 = w.reshape(k, -1, d)
    return jnp.pad(w3, ((0, 0), (0, 0), (lead, HP - d - lead))).reshape(k, -1)


def _inv_freq(d):
    return ROPE_THETA ** (-jnp.arange(0, d, 2, dtype=F32) / d)


def _layer_weights(l, w_in, w_uq, w_ukv, lru_wa, lru_ba, lru_wx, lru_bx, grp_norm, w_out):
    wi = w_in[l]
    c_q, c_kv, k_r = wi[:, 0:256], wi[:, 256:384], wi[:, 384:416]
    s_q, s_k, s_v = wi[:, 416:672], wi[:, 672:800], wi[:, 800:928]
    l_x, l_g = wi[:, 928:1312], wi[:, 1312:1696]
    k_r_blk = jnp.concatenate([jnp.zeros((D_MODEL, MLA_NOPE), F32), k_r, _rot_half_cols(k_r, MLA_ROPE)], axis=1)
    w_in_p = jnp.concatenate([
        c_q, c_kv,
        _pad_heads(s_q, SWA_HEAD_DIM), _pad_heads(_rot_half_cols(s_q, SWA_HEAD_DIM), SWA_HEAD_DIM),
        _pad_heads(s_k, SWA_HEAD_DIM), _pad_heads(_rot_half_cols(s_k, SWA_HEAD_DIM), SWA_HEAD_DIM),
        _pad_heads(s_v, SWA_HEAD_DIM), l_x, l_g, k_r_blk], axis=1).astype(BF16)

    wq3 = w_uq[l].reshape(MLA_Q_RANK, MLA_HEADS, MLA_NOPE + MLA_ROPE)
    q_nope = wq3[..., :MLA_NOPE].reshape(MLA_Q_RANK, -1)
    q_rope = wq3[..., MLA_NOPE:].reshape(MLA_Q_RANK, -1)
    wq = jnp.concatenate([
        _pad_heads(q_nope, MLA_NOPE) + _pad_heads(q_rope, MLA_ROPE, lead=MLA_NOPE),
        _pad_heads(_rot_half_cols(q_rope, MLA_ROPE), MLA_ROPE, lead=MLA_NOPE)], axis=1).astype(BF16)

    wkv3 = w_ukv[l].reshape(MLA_KV_RANK, MLA_HEADS, MLA_NOPE + MLA_V)
    wkv = jnp.concatenate([
        _pad_heads(wkv3[..., :MLA_NOPE].reshape(MLA_KV_RANK, -1), MLA_NOPE),
        _pad_heads(wkv3[..., MLA_NOPE:].reshape(MLA_KV_RANK, -1), MLA_V)], axis=1).astype(BF16)

    def blockdiag(w):
        n, bi, bj = w.shape
        eye = jnp.eye(n, dtype=w.dtype)
        return (w[:, :, None, :] * eye[:, None, :, None]).reshape(n * bi, n * bj)

    w_gate = [jnp.concatenate([blockdiag(lru_wa[l, d]), blockdiag(lru_wx[l, d])], axis=1).astype(BF16) for d in range(2)]
    b_gate = [jnp.concatenate([lru_ba[l, d].reshape(1, -1), lru_bx[l, d].reshape(1, -1)], axis=1) for d in range(2)]

    g = grp_norm[l]
    wo = w_out[l]
    ga, gb = MLA_HEADS * MLA_V, MLA_HEADS * MLA_V + LRU_WIDTH

    def pad_rows(w, d):
        return jnp.pad(w.reshape(-1, d, w.shape[-1]), ((0, 0), (0, HP - d), (0, 0))).reshape(-1, w.shape[-1])

    out_w = dict(
        g_a=_pad_heads(g[None, :ga], MLA_V), g_b=g[None, ga:gb], g_c=_pad_heads(g[None, gb:], SWA_HEAD_DIM),
        w_a=pad_rows(wo[:ga], MLA_V).astype(BF16), w_b=wo[ga:gb].astype(BF16),
        w_c=pad_rows(wo[gb:], SWA_HEAD_DIM).astype(BF16))
    return w_in_p, wq, wkv, w_gate, b_gate, out_w


def _rope_table_kernel(pos_ref, inv_ref, cos_ref, sin_ref):
    ang = pos_ref[...] * inv_ref[...]
    cos_ref[...] = jnp.cos(ang)
    sin_ref[...] = jnp.sin(ang)


def _rope_tables(pos, inv, tm=512):
    t, w = pos.shape[0], inv.shape[1]
    return pl.pallas_call(
        _rope_table_kernel, name="rope_tables",
        grid=(t // tm,),
        in_specs=[pl.BlockSpec((tm, 1), lambda i: (i, 0)), pl.BlockSpec((1, w), lambda i: (0, 0))],
        out_specs=[pl.BlockSpec((tm, w), lambda i: (i, 0))] * 2,
        out_shape=[jax.ShapeDtypeStruct((t, w), F32)] * 2,
        compiler_params=_cparams("parallel"),
    )(pos, inv)


def _in_proj_kernel(x_ref, nmix_ref, win_ref, qn_ref, wq_ref, kvn_ref, wkv_ref, cosq_ref, sinq_ref, coss_ref, sins_ref,
                    q_out, k_out, v_out, sq_out, sk_out, sv_out, lx_out, lg_out):
    xn = _rmsnorm(x_ref[...], nmix_ref[...], D_MODEL)
    proj = jnp.dot(xn.astype(BF16), win_ref[...], preferred_element_type=F32)
    cosq, sinq, coss, sins = cosq_ref[...], sinq_ref[...], coss_ref[...], sins_ref[...]

    swa_scale = SWA_HEAD_DIM ** -0.5
    sq_out[...] = ((proj[:, _S_Q:_S_QR] * coss + proj[:, _S_QR:_S_K] * sins) * swa_scale).astype(BF16)
    sk_out[...] = (proj[:, _S_K:_S_KR] * coss[:, :SWA_KV] + proj[:, _S_KR:_S_V] * sins[:, :SWA_KV]).astype(BF16)
    sv_out[...] = proj[:, _S_V:_L_X].astype(BF16)
    lx_out[...] = proj[:, _L_X:_L_G]
    lg_out[...] = proj[:, _L_G:_K_R]

    cqn = _rmsnorm(proj[:, _C_Q:_C_KV], qn_ref[...], MLA_Q_RANK).astype(BF16)
    qq = jnp.dot(cqn, wq_ref[...], preferred_element_type=F32)
    mla_scale = (MLA_NOPE + MLA_ROPE) ** -0.5
    q_out[...] = ((qq[:, :MLA_QK] * cosq + qq[:, MLA_QK:] * sinq) * mla_scale).astype(BF16)

    kvn = _rmsnorm(proj[:, _C_KV:_S_Q], kvn_ref[...], MLA_KV_RANK).astype(BF16)
    kv = jnp.dot(kvn, wkv_ref[...], preferred_element_type=F32)
    kr = proj[:, _K_R:_IN_COLS]
    lane = lax.broadcasted_iota(I32, kr.shape, 1)
    in_rope = (lane >= MLA_NOPE) & (lane < MLA_NOPE + MLA_ROPE)
    kr_rot = jnp.where(in_rope, kr * cosq[:, :HP] + pltpu.roll(kr, HP - MLA_ROPE, axis=1) * sinq[:, :HP], 0.0)
    for h in range(MLA_HEADS):
        k_out[:, h * HP:(h + 1) * HP] = (kv[:, h * HP:(h + 1) * HP] + kr_rot).astype(BF16)
    v_out[...] = kv[:, MLA_QK:].astype(BF16)


def _in_proj(x, nmix, w_in_p, qn, wq, kvn, wkv, cosq, sinq, coss, sins, tm=256):
    t = x.shape[0]
    row = lambda w: pl.BlockSpec((tm, w), lambda i: (i, 0))
    full = lambda a: pl.BlockSpec(a.shape, lambda i: (0,) * a.ndim)
    widths = (MLA_QK, MLA_QK, MLA_QK, SWA_Q, SWA_KV, SWA_KV, LRU_WIDTH, LRU_WIDTH)
    dtypes = (BF16, BF16, BF16, BF16, BF16, BF16, F32, F32)
    return pl.pallas_call(
        _in_proj_kernel, name="in_proj",
        grid=(t // tm,),
        in_specs=[row(D_MODEL), full(nmix), full(w_in_p), full(qn), full(wq), full(kvn), full(wkv),
                  row(MLA_QK), row(MLA_QK), row(SWA_Q), row(SWA_Q)],
        out_specs=[row(w) for w in widths],
        out_shape=[jax.ShapeDtypeStruct((t, w), d) for w, d in zip(widths, dtypes)],
        compiler_params=_cparams("parallel"),
    )(x, nmix, w_in_p, qn, wq, kvn, wkv, cosq, sinq, coss, sins)


def _mla_kernel(q_ref, k_ref, v_ref, o_ref, *, tk):
    q = q_ref[...]
    tq = q.shape[0]
    nk = k_ref.shape[0] // tk

    def body(j, carry):
        m, l, acc = carry
        off = pl.multiple_of(j * tk, tk)
        k = k_ref[pl.ds(off, tk), :]
        v = v_ref[pl.ds(off, tk), :]
        s = lax.dot_general(q, k, (((1,), (1,)), ((), ())), preferred_element_type=F32)
        m_new = jnp.maximum(m, jnp.max(s, axis=-1, keepdims=True))
        alpha = jnp.exp(m - m_new)
        p = jnp.exp(s - m_new)
        l = alpha * l + jnp.sum(p, axis=-1, keepdims=True)
        acc = alpha * acc + jnp.dot(p.astype(BF16), v, preferred_element_type=F32)
        return m_new, l, acc

    init = (jnp.full((tq, 1), -jnp.inf, F32), jnp.zeros((tq, 1), F32), jnp.zeros((tq, HP), F32))
    _, l, acc = lax.fori_loop(0, nk, body, init)
    o_ref[...] = acc / l


def _mla_attention(q, k, v, batch, seq, tq=512, tk=512):
    nq = seq // tq
    return pl.pallas_call(
        functools.partial(_mla_kernel, tk=tk), name="mla_attention",
        grid=(batch, MLA_HEADS, nq),
        in_specs=[pl.BlockSpec((tq, HP), lambda b, h, i: (b * nq + i, h)),
                  pl.BlockSpec((seq, HP), lambda b, h, i: (b, h)),
                  pl.BlockSpec((seq, HP), lambda b, h, i: (b, h))],
        out_specs=pl.BlockSpec((tq, HP), lambda b, h, i: (b * nq + i, h)),
        out_shape=jax.ShapeDtypeStruct((batch * seq, MLA_QK), F32),
        compiler_params=_cparams("parallel", "parallel", "arbitrary"),
    )(q, k, v)


def _swa_kernel(sink_ref, q_ref, kp_ref, kc_ref, kn_ref, vp_ref, vc_ref, vn_ref, o_ref, *, seq):
    w = SWA_WINDOW
    tq = q_ref.shape[0]
    i = pl.program_id(1)
    kcat = jnp.concatenate([kp_ref[...], kc_ref[...], kn_ref[...]], axis=0)
    vcat = jnp.concatenate([vp_ref[...], vc_ref[...], vn_ref[...]], axis=0)
    for j in range(tq // w):
        qpos = i * tq + j * w + lax.broadcasted_iota(I32, (w, 3 * w), 0)
        kpos = i * tq + (j - 1) * w + lax.broadcasted_iota(I32, (w, 3 * w), 1)
        valid = (jnp.abs(kpos - qpos) <= w) & (kpos >= 0) & (kpos < seq)
        for h in range(SWA_Q_HEADS):
            kh = h // (SWA_Q_HEADS // SWA_KV_HEADS)
            qh = q_ref[j * w:(j + 1) * w, h * HP:(h + 1) * HP]
            kj = kcat[j * w:(j + 3) * w, kh * HP:(kh + 1) * HP]
            vj = vcat[j * w:(j + 3) * w, kh * HP:(kh + 1) * HP]
            s = lax.dot_general(qh, kj, (((1,), (1,)), ((), ())), preferred_element_type=F32)
            s = jnp.where(valid, s, NEG_BIG)
            sink = sink_ref[0, h]
            m = jnp.maximum(jnp.max(s, axis=-1, keepdims=True), sink)
            p = jnp.exp(s - m)
            den = jnp.sum(p, axis=-1, keepdims=True) + jnp.exp(sink - m)
            o_ref[j * w:(j + 1) * w, h * HP:(h + 1) * HP] = jnp.dot(p.astype(BF16), vj, preferred_element_type=F32) / den


def _swa_attention(sq, sk, sv, sink, batch, seq, tq=512):
    w = SWA_WINDOW
    nq, nw, r = seq // tq, seq // w, tq // w
    prev = pl.BlockSpec((w, SWA_KV), lambda b, i: (b * nw + jnp.maximum(i * r - 1, 0), 0))
    cur = pl.BlockSpec((tq, SWA_KV), lambda b, i: (b * nq + i, 0))
    nxt = pl.BlockSpec((w, SWA_KV), lambda b, i: (b * nw + jnp.minimum((i + 1) * r, nw - 1), 0))
    return pl.pallas_call(
        functools.partial(_swa_kernel, seq=seq), name="swa_attention",
        grid=(batch, nq),
        in_specs=[pl.BlockSpec(memory_space=pltpu.SMEM),
                  pl.BlockSpec((tq, SWA_Q), lambda b, i: (b * nq + i, 0)),
                  prev, cur, nxt, prev, cur, nxt],
        out_specs=pl.BlockSpec((tq, SWA_Q), lambda b, i: (b * nq + i, 0)),
        out_shape=jax.ShapeDtypeStruct((batch * seq, SWA_Q), F32),
        compiler_params=_cparams("parallel", "parallel"),
    )(sink, sq, sk, sk, sk, sv, sv, sv)


def _lru_kernel(xpf_ref, xcf_ref, xnf_ref, xpb_ref, xcb_ref, xnb_ref, cw_ref, cb_ref, wf_ref, bf_ref, wb_ref, bb_ref,
                lam_ref, hf_out, hb_out, carry_f, carry_b, *, nt):
    tm = xcf_ref.shape[0]
    i = pl.program_id(1)
    halo = xpf_ref.shape[0]

    @pl.when(i == 0)
    def _():
        carry_f[...] = jnp.zeros_like(carry_f)
        carry_b[...] = jnp.zeros_like(carry_b)

    def gates(xp_ref, xc_ref, xn_ref, first, last, w_ref, b_ref, lam):
        prev = jnp.where(first, 0.0, xp_ref[...])
        nxt = jnp.where(last, 0.0, xn_ref[...])
        xcat = jnp.concatenate([prev, xc_ref[...], nxt], axis=0)
        cw = cw_ref[...]
        conv = cb_ref[...]
        for tap in range(cw.shape[0]):
            conv = conv + cw[tap:tap + 1, :] * xcat[halo - 1 + tap: halo - 1 + tap + tm, :]
        g = jnp.dot(conv.astype(BF16), w_ref[...], preferred_element_type=F32) + b_ref[...]
        r = _sigmoid(g[:, :LRU_WIDTH])
        gate_i = _sigmoid(g[:, LRU_WIDTH:])
        softplus = jnp.maximum(-lam, 0.0) + jnp.log1p(jnp.exp(-jnp.abs(lam)))
        log_a = -LRU_C * r * softplus
        a = jnp.exp(log_a)
        b = jnp.sqrt(1.0 - a * a) * (gate_i * conv)
        return a, b

    row = lax.broadcasted_iota(I32, (tm, LRU_WIDTH), 0)

    def scan(a, b, reverse):
        k = 1
        while k < tm:
            if reverse:
                keep = row < tm - k
                shift = tm - k
            else:
                keep = row >= k
                shift = k
            a_s = jnp.where(keep, pltpu.roll(a, shift, axis=0), 1.0)
            b_s = jnp.where(keep, pltpu.roll(b, shift, axis=0), 0.0)
            b = a * b_s + b
            a = a * a_s
            k *= 2
        return a, b

    a, b = gates(xpf_ref, xcf_ref, xnf_ref, i == 0, i == nt - 1, wf_ref, bf_ref, lam_ref[0:1, :])
    a, b = scan(a, b, False)
    h = a * carry_f[...] + b
    hf_out[...] = h
    carry_f[...] = h[tm - 1:tm, :]

    a, b = gates(xpb_ref, xcb_ref, xnb_ref, i == nt - 1, i == 0, wb_ref, bb_ref, lam_ref[1:2, :])
    a, b = scan(a, b, True)
    h = a * carry_b[...] + b
    hb_out[...] = h
    carry_b[...] = h[0:1, :]


def _lru_scan(lx, conv_w, conv_b, w_gate, b_gate, lam, batch, seq, tm=256):
    nt, hb = seq // tm, seq // SUBLANES
    r = tm // SUBLANES
    fwd = lambda b, i: i
    bwd = lambda b, i: nt - 1 - i

    def specs(tile):
        return [pl.BlockSpec((SUBLANES, LRU_WIDTH), lambda b, i: (b * hb + jnp.maximum(tile(b, i) * r - 1, 0), 0)),
                pl.BlockSpec((tm, LRU_WIDTH), lambda b, i: (b * nt + tile(b, i), 0)),
                pl.BlockSpec((SUBLANES, LRU_WIDTH), lambda b, i: (b * hb + jnp.minimum((tile(b, i) + 1) * r, hb - 1), 0))]

    full = lambda a: pl.BlockSpec(a.shape, lambda b, i: (0,) * a.ndim)
    return pl.pallas_call(
        functools.partial(_lru_kernel, nt=nt), name="lru_scan",
        grid=(batch, nt),
        in_specs=specs(fwd) + specs(bwd) + [full(conv_w), full(conv_b), full(w_gate[0]), full(b_gate[0]),
                                            full(w_gate[1]), full(b_gate[1]), full(lam)],
        out_specs=[pl.BlockSpec((tm, LRU_WIDTH), lambda b, i: (b * nt + i, 0)),
                   pl.BlockSpec((tm, LRU_WIDTH), lambda b, i: (b * nt + nt - 1 - i, 0))],
        out_shape=[jax.ShapeDtypeStruct((batch * seq, LRU_WIDTH), F32)] * 2,
        scratch_shapes=[pltpu.VMEM((1, LRU_WIDTH), F32), pltpu.VMEM((1, LRU_WIDTH), F32)],
        compiler_params=_cparams("parallel", "arbitrary"),
    )(lx, lx, lx, lx, lx, lx, conv_w, conv_b, w_gate[0], b_gate[0], w_gate[1], b_gate[1], lam)


def _out_proj_kernel(x_ref, oa_ref, hf_ref, hb_ref, lg_ref, oc_ref, ga_ref, gb_ref, gc_ref, wa_ref, wb_ref, wc_ref,
                     nffn_ref, wq_ref, x1_out, xn_out, q_out):
    mix_a = _rmsnorm(oa_ref[...], ga_ref[...], MLA_HEADS * MLA_V).astype(BF16)
    o_lru = (hf_ref[...] + hb_ref[...]) * _gelu(lg_ref[...])
    mix_b = _rmsnorm(o_lru, gb_ref[...], LRU_WIDTH).astype(BF16)
    mix_c = _rmsnorm(oc_ref[...], gc_ref[...], SWA_Q_HEADS * SWA_HEAD_DIM).astype(BF16)
    x1 = (x_ref[...]
          + jnp.dot(mix_a, wa_ref[...], preferred_element_type=F32)
          + jnp.dot(mix_b, wb_ref[...], preferred_element_type=F32)
          + jnp.dot(mix_c, wc_ref[...], preferred_element_type=F32))
    x1_out[...] = x1
    xn = _rmsnorm(x1, nffn_ref[...], D_MODEL)
    xn_out[...] = xn
    q_out[...] = jnp.dot(xn.astype(BF16), wq_ref[...], preferred_element_type=F32)


def _out_proj(x, o_mla, h_f, h_b, l_g, o_swa, ow, nffn, wq, tm=256):
    t = x.shape[0]
    row = lambda w: pl.BlockSpec((tm, w), lambda i: (i, 0))
    full = lambda a: pl.BlockSpec(a.shape, lambda i: (0,) * a.ndim)
    nq = wq.shape[1]
    return pl.pallas_call(
        _out_proj_kernel, name="out_proj",
        grid=(t // tm,),
        in_specs=[row(D_MODEL), row(MLA_QK), row(LRU_WIDTH), row(LRU_WIDTH), row(LRU_WIDTH), row(SWA_Q),
                  full(ow["g_a"]), full(ow["g_b"]), full(ow["g_c"]), full(ow["w_a"]), full(ow["w_b"]), full(ow["w_c"]),
                  full(nffn), full(wq)],
        out_specs=[row(D_MODEL), row(D_MODEL), row(nq)],
        out_shape=[jax.ShapeDtypeStruct((t, D_MODEL), F32), jax.ShapeDtypeStruct((t, D_MODEL), F32),
                   jax.ShapeDtypeStruct((t, nq), F32)],
        compiler_params=_cparams("parallel"),
    )(x, o_mla, h_f, h_b, l_g, o_swa, ow["g_a"], ow["g_b"], ow["g_c"], ow["w_a"], ow["w_b"], ow["w_c"], nffn, wq)


def _top_k_rows(s, k, payload=None):
    n = s.shape[0]
    row = lax.broadcasted_iota(I32, s.shape, 0)
    vals, picks = [], []
    for _ in range(k):
        m = jnp.max(s, axis=0, keepdims=True)
        idx = jnp.min(jnp.where(s == m, row, n), axis=0, keepdims=True)
        hit = row == idx
        vals.append(m)
        picks.append(idx if payload is None else jnp.sum(jnp.where(hit, payload, 0), axis=0, keepdims=True))
        s = jnp.where(hit, -jnp.inf, s)
    return jnp.concatenate(vals, axis=0), jnp.concatenate(picks, axis=0)


def _peer_topk_kernel(q_ref, keys_ref, id_out, gate_out):
    half = PEER_DKEY // 2
    dn = (((1,), (1,)), ((), ()))
    q = q_ref[...]
    s1 = lax.dot_general(keys_ref[0], q[:, :half], dn, preferred_element_type=F32, precision=lax.Precision.HIGHEST)
    s2 = lax.dot_general(keys_ref[1], q[:, half:], dn, preferred_element_type=F32, precision=lax.Precision.HIGHEST)
    v1, i1 = _top_k_rows(s1, PEER_TOPK)
    v2, i2 = _top_k_rows(s2, PEER_TOPK)
    cand = jnp.concatenate([v1[a:a + 1, :] + v2 for a in range(PEER_TOPK)], axis=0)
    cand_id = jnp.concatenate([i1[a:a + 1, :] * PEER_NKEYS + i2 for a in range(PEER_TOPK)], axis=0)
    sc, ids = _top_k_rows(cand, PEER_TOPK, payload=cand_id)
    e = jnp.exp(sc - sc[0:1, :])
    id_out[0] = ids
    gate_out[0] = e / jnp.sum(e, axis=0, keepdims=True)


def _peer_topk(q, subkeys, tm=256):
    t = q.shape[0]
    out = pl.BlockSpec((1, PEER_TOPK, tm), lambda i, h: (h, 0, i))
    return pl.pallas_call(
        _peer_topk_kernel, name="peer_topk",
        grid=(t // tm, PEER_HEADS),
        in_specs=[pl.BlockSpec((tm, PEER_DKEY), lambda i, h: (i, h)),
                  pl.BlockSpec(subkeys.shape, lambda i, h: (0, 0, 0))],
        out_specs=[out, out],
        out_shape=[jax.ShapeDtypeStruct((PEER_HEADS, PEER_TOPK, t), I32),
                   jax.ShapeDtypeStruct((PEER_HEADS, PEER_TOPK, t), F32)],
        compiler_params=_cparams("parallel", "parallel"),
    )(q, subkeys)


def _fold_sublanes(vals, masks):
    for level, mask in enumerate(masks):
        nxt = []
        for a, b in zip(vals[0::2], vals[1::2]):
            nxt.append(jnp.where(mask, a, b) + pltpu.roll(jnp.where(mask, b, a), 1 << level, axis=0))
        vals = nxt
    (out,) = vals
    return out


def _peer_hidden_kernel(id_ref, x_ref, u_ref, h_out, r_scr):
    tm = x_ref.shape[0]
    sub = lax.broadcasted_iota(I32, (SUBLANES, LANES), 0)
    sub_masks = [(sub % (2 << lv)) < (1 << lv) for lv in range(3)]
    groups = PEER_SLOTS // SUBLANES
    ones = jnp.ones((SUBLANES, LANES), BF16)
    dn = (((1,), (1,)), ((), ()))

    def token(tt, g):
        t = g * SUBLANES + tt
        xt = x_ref[t]
        for j in range(groups):
            prods = []
            for s in range(SUBLANES):
                row = pl.multiple_of(id_ref[t * PEER_SLOTS + j * SUBLANES + s], SUBLANES)
                prods.append(u_ref[pl.ds(row, SUBLANES), :] * xt)
            r_scr[pl.ds(pl.multiple_of(tt * PEER_SLOTS + j * SUBLANES, SUBLANES), SUBLANES), :] = _fold_sublanes(prods, sub_masks)
        return g

    def group(g, _):
        lax.fori_loop(0, SUBLANES, token, g)
        r = r_scr[...]
        hi = r.astype(BF16)
        lo = (r - hi.astype(F32)).astype(BF16)
        sums = (lax.dot_general(ones, hi, dn, preferred_element_type=F32)
                + lax.dot_general(ones, lo, dn, preferred_element_type=F32))
        out = jnp.zeros((SUBLANES, LANES), F32)
        for tt in range(SUBLANES):
            out = jnp.where(sub == tt, sums[:, tt * PEER_SLOTS:(tt + 1) * PEER_SLOTS], out)
        h_out[pl.ds(pl.multiple_of(g * SUBLANES, SUBLANES), SUBLANES), :] = out
        return 0

    lax.fori_loop(0, tm // SUBLANES, group, 0)


def _smem_rows(tm, nt):
    return pl.BlockSpec((tm * PEER_SLOTS,), lambda p, i: (p * nt + i,), memory_space=pltpu.SMEM)


def _peer_hidden(row8, x3, u2, tm=128):
    t = x3.shape[0]
    return pl.pallas_call(
        _peer_hidden_kernel, name="peer_hidden",
        grid=(TABLE_HALVES, t // tm),
        in_specs=[_smem_rows(tm, t // tm),
                  pl.BlockSpec((tm, SUBLANES, LANES), lambda p, i: (i, 0, 0)),
                  pl.BlockSpec((HALF_EXPERTS * SUBLANES, LANES), lambda p, i: (p, 0), pipeline_mode=pl.Buffered(1))],
        out_specs=pl.BlockSpec((None, tm, PEER_SLOTS), lambda p, i: (p, i, 0)),
        out_shape=jax.ShapeDtypeStruct((TABLE_HALVES, t, PEER_SLOTS), F32),
        scratch_shapes=[pltpu.VMEM((SUBLANES * PEER_SLOTS, LANES), F32)],
        compiler_params=_cparams("arbitrary", "arbitrary"),
    )(row8, x3, u2)


def _peer_coef_kernel(h_ref, gate_ref, id_ref, c_out):
    low = id_ref[...] < HALF_EXPERTS
    c = gate_ref[...] * _gelu(jnp.where(low, h_ref[0], h_ref[1]))
    for p in range(TABLE_HALVES):
        ct = jnp.where(low if p == 0 else jnp.logical_not(low), c, 0.0).T
        for g in range(c.shape[0] // SUBLANES):
            c_out[p, g] = ct if g == 0 else pltpu.roll(ct, LANES - SUBLANES * g, axis=1)


def _peer_coef(hid, gate, ids, tm=LANES):
    t = gate.shape[0]
    gpt = tm // SUBLANES
    return pl.pallas_call(
        _peer_coef_kernel, name="peer_coef",
        grid=(t // tm,),
        in_specs=[pl.BlockSpec((TABLE_HALVES, tm, PEER_SLOTS), lambda i: (0, i, 0)),
                  pl.BlockSpec((tm, PEER_SLOTS), lambda i: (i, 0)),
                  pl.BlockSpec((tm, PEER_SLOTS), lambda i: (i, 0))],
        out_specs=pl.BlockSpec((TABLE_HALVES, gpt, PEER_SLOTS, LANES), lambda i: (0, i, 0, 0)),
        out_shape=jax.ShapeDtypeStruct((TABLE_HALVES, t // SUBLANES, PEER_SLOTS, LANES), F32),
        compiler_params=_cparams("parallel"),
    )(hid, gate, ids)


def _peer_value_kernel(id_ref, c_ref, v_ref, y_out, cb_scr):
    tm = y_out.shape[0]

    def group(g, _):
        cg = c_ref[g]
        for tt in range(SUBLANES):
            cb_scr[tt * PEER_SLOTS:(tt + 1) * PEER_SLOTS, :] = jnp.broadcast_to(cg[:, tt:tt + 1], (PEER_SLOTS, LANES))

        def token(tt, _):
            t = g * SUBLANES + tt

            acc = jnp.zeros((SUBLANES, LANES), F32)
            for j in range(PEER_SLOTS // SUBLANES):
                terms = []
                for s in range(SUBLANES):
                    e = j * SUBLANES + s
                    row = pl.multiple_of(id_ref[t * PEER_SLOTS + e], SUBLANES)
                    c = jnp.broadcast_to(cb_scr[pl.ds(tt * PEER_SLOTS + e, 1), :], (SUBLANES, LANES))
                    terms.append(c * v_ref[pl.ds(row, SUBLANES), :])
                while len(terms) > 1:
                    terms = [a + b for a, b in zip(terms[0::2], terms[1::2])]
                acc = acc + terms[0]
            y_out[t] = acc
            return 0

        lax.fori_loop(0, SUBLANES, token, 0)
        return 0

    lax.fori_loop(0, tm // SUBLANES, group, 0)


def _peer_value(row8, coef, v2, tm=128):
    t = coef.shape[1] * SUBLANES
    return pl.pallas_call(
        _peer_value_kernel, name="peer_value",
        grid=(TABLE_HALVES, t // tm),
        in_specs=[_smem_rows(tm, t // tm),
                  pl.BlockSpec((None, tm // SUBLANES, PEER_SLOTS, LANES), lambda p, i: (p, i, 0, 0)),
                  pl.BlockSpec((HALF_EXPERTS * SUBLANES, LANES), lambda p, i: (p, 0), pipeline_mode=pl.Buffered(1))],
        out_specs=pl.BlockSpec((None, tm, SUBLANES, LANES), lambda p, i: (p, i, 0, 0)),
        out_shape=jax.ShapeDtypeStruct((TABLE_HALVES, t, SUBLANES, LANES), F32),
        scratch_shapes=[pltpu.VMEM((SUBLANES * PEER_SLOTS, LANES), F32)],
        compiler_params=_cparams("arbitrary", "arbitrary"),
    )(row8, coef, v2)


def _residual_kernel(x_ref, y_ref, g_ref, o_ref, *, final):
    x = x_ref[...] + y_ref[0] + y_ref[1]
    o_ref[...] = _rmsnorm(x, g_ref[...], D_MODEL) if final else x


def _residual(x1, y, gain, final, tm=512):
    t = x1.shape[0]
    return pl.pallas_call(
        functools.partial(_residual_kernel, final=final), name="peer_residual",
        grid=(t // tm,),
        in_specs=[pl.BlockSpec((tm, D_MODEL), lambda i: (i, 0)),
                  pl.BlockSpec((TABLE_HALVES, tm, D_MODEL), lambda i: (0, i, 0)),
                  pl.BlockSpec((1, D_MODEL), lambda i: (0, 0))],
        out_specs=pl.BlockSpec((tm, D_MODEL), lambda i: (i, 0)),
        out_shape=jax.ShapeDtypeStruct((t, D_MODEL), F32),
        compiler_params=_cparams("parallel"),
    )(x1, y, gain)


def _peer_ffn(x1, xn, q, subkeys, u, v, gain, final):
    t = x1.shape[0]
    ids_h, gate_h = _peer_topk(q, subkeys)
    ids = ids_h.reshape(PEER_SLOTS, t).T
    gate = gate_h.reshape(PEER_SLOTS, t).T
    base = (jnp.arange(TABLE_HALVES, dtype=I32) * HALF_EXPERTS)[:, None, None]
    row8 = (jnp.clip(ids[None] - base, 0, HALF_EXPERTS - 1) * SUBLANES).reshape(-1)
    hid = _peer_hidden(row8, xn.reshape(t, SUBLANES, LANES), u.reshape(PEER_EXPERTS * SUBLANES, LANES))
    coef = _peer_coef(hid, gate, ids)
    y = _peer_value(row8, coef, v.reshape(PEER_EXPERTS * SUBLANES, LANES))
    return _residual(x1, y.reshape(TABLE_HALVES, t, D_MODEL), gain, final)


def kernel(x, positions, norm_mix, w_in, q_norm, w_uq, kv_norm, w_ukv, conv_w, conv_b, lru_wa, lru_ba, lru_wx, lru_bx,
           lru_lambda, swa_sink, grp_norm, w_out, norm_ffn, peer_wq, peer_subkeys, peer_u, peer_v, norm_final):
    batch, seq, _ = x.shape
    t = batch * seq
    depth = w_in.shape[0]
    xt = x.reshape(t, D_MODEL)

    pos = positions.astype(F32).reshape(t, 1)
    zeros = lambda n: jnp.zeros((n,), F32)
    inv_m, inv_s = _inv_freq(MLA_ROPE), _inv_freq(SWA_HEAD_DIM)
    inv_q = jnp.tile(jnp.concatenate([zeros(MLA_NOPE), inv_m, inv_m, zeros(HP - MLA_NOPE - MLA_ROPE)]), MLA_HEADS)
    inv_w = jnp.tile(jnp.concatenate([inv_s, inv_s, zeros(HP - SWA_HEAD_DIM)]), SWA_Q_HEADS)
    cosq, sinq = _rope_tables(pos, inv_q[None, :])
    coss, sins = _rope_tables(pos, inv_w[None, :])

    for l in range(depth):
        w_in_p, wq, wkv, w_gate, b_gate, ow = _layer_weights(
            l, w_in, w_uq, w_ukv, lru_wa, lru_ba, lru_wx, lru_bx, grp_norm, w_out)
        q, k, v, sq, sk, sv, lx, lg = _in_proj(
            xt, norm_mix[l][None, :], w_in_p, q_norm[l][None, :], wq, kv_norm[l][None, :], wkv, cosq, sinq, coss, sins)
        o_mla = _mla_attention(q, k, v, batch, seq)
        o_swa = _swa_attention(sq, sk, sv, swa_sink[l][None, :], batch, seq)
        h_f, h_b = _lru_scan(lx, conv_w[l], conv_b[l][None, :], w_gate, b_gate, lru_lambda[l], batch, seq)
        x1, xn, pq = _out_proj(xt, o_mla, h_f, h_b, lg, o_swa, ow, norm_ffn[l][None, :], peer_wq[l].astype(BF16))
        final = l == depth - 1
        gain = norm_final[None, :] if final else jnp.ones((1, D_MODEL), F32)
        xt = _peer_ffn(x1, xn, pq, peer_subkeys[l], peer_u[l], peer_v[l], gain, final)
    return xt.reshape(batch, seq, D_MODEL)
```

```python
import functools

import jax
import jax.numpy as jnp
from jax import lax
from jax.experimental import pallas as pl
from jax.experimental.pallas import tpu as pltpu

F32 = jnp.float32
BF16 = jnp.bfloat16
I32 = jnp.int32

D_MODEL = 1024
MLA_HEADS, MLA_NOPE, MLA_ROPE, MLA_V = 6, 64, 32, 64
MLA_Q_RANK, MLA_KV_RANK = 256, 128
LRU_WIDTH, LRU_BLOCKS, LRU_C = 384, 6, 8.0
SWA_Q_HEADS, SWA_KV_HEADS, SWA_HEAD_DIM, SWA_WINDOW = 4, 2, 64, 128
ROPE_THETA, EPS, NEG_BIG = 10000.0, 1e-6, -1e30
PEER_HEADS, PEER_NKEYS, PEER_DKEY, PEER_TOPK = 8, 128, 256, 16
PEER_SLOTS = PEER_HEADS * PEER_TOPK
PEER_EXPERTS = PEER_NKEYS * PEER_NKEYS

LANES = 128
SUBLANES = 8
HP = LANES
MLA_QK = MLA_HEADS * HP
SWA_Q = SWA_Q_HEADS * HP
SWA_KV = SWA_KV_HEADS * HP
VMEM_LIMIT = 56 * 1024 * 1024
PACK_ROWS = D_MODEL // (2 * LANES)
TABLE_ROWS = PEER_EXPERTS * PACK_ROWS + SUBLANES

_C_Q, _C_KV, _S_Q, _S_QR, _S_K, _S_KR, _S_V, _L_X, _L_G, _K_R, _IN_COLS = (
    0, 256, 384, 896, 1408, 1664, 1920, 2176, 2560, 2944, 3072)


def _cparams(*sem):
    return pltpu.CompilerParams(dimension_semantics=sem, vmem_limit_bytes=VMEM_LIMIT)


def _rmsnorm(x, g, n):
    return x * lax.rsqrt(jnp.sum(x * x, axis=-1, keepdims=True) * (1.0 / n) + EPS) * g


def _sigmoid(x):
    return 1.0 / (1.0 + jnp.exp(-x))


def _gelu(x):
    return 0.5 * x * (1.0 + lax.erf(x * (2.0 ** -0.5)))


def _rot_half_cols(w, d):
    k = w.shape[0]
    w3 = w.reshape(k, -1, d)
    return jnp.concatenate([-w3[..., d // 2:], w3[..., : d // 2]], axis=-1).reshape(k, -1)


def _pad_heads(w, d, lead=0):
    k = w.shape[0]
    w3 = w.reshape(k, -1, d)
    return jnp.pad(w3, ((0, 0), (0, 0), (lead, HP - d - lead))).reshape(k, -1)


def _inv_freq(d):
    return ROPE_THETA ** (-jnp.arange(0, d, 2, dtype=F32) / d)


def _layer_weights(l, w_in, w_uq, w_ukv, lru_wa, lru_ba, lru_wx, lru_bx, grp_norm, w_out):
    wi = w_in[l]
    c_q, c_kv, k_r = wi[:, 0:256], wi[:, 256:384], wi[:, 384:416]
    s_q, s_k, s_v = wi[:, 416:672], wi[:, 672:800], wi[:, 800:928]
    l_x, l_g = wi[:, 928:1312], wi[:, 1312:1696]
    k_r_blk = jnp.concatenate([jnp.zeros((D_MODEL, MLA_NOPE), F32), k_r, _rot_half_cols(k_r, MLA_ROPE)], axis=1)
    w_in_p = jnp.concatenate([
        c_q, c_kv,
        _pad_heads(s_q, SWA_HEAD_DIM), _pad_heads(_rot_half_cols(s_q, SWA_HEAD_DIM), SWA_HEAD_DIM),
        _pad_heads(s_k, SWA_HEAD_DIM), _pad_heads(_rot_half_cols(s_k, SWA_HEAD_DIM), SWA_HEAD_DIM),
        _pad_heads(s_v, SWA_HEAD_DIM), l_x, l_g, k_r_blk], axis=1).astype(BF16)

    wq3 = w_uq[l].reshape(MLA_Q_RANK, MLA_HEADS, MLA_NOPE + MLA_ROPE)
    q_nope = wq3[..., :MLA_NOPE].reshape(MLA_Q_RANK, -1)
    q_rope = wq3[..., MLA_NOPE:].reshape(MLA_Q_RANK, -1)
    wq = jnp.concatenate([
        _pad_heads(q_nope, MLA_NOPE) + _pad_heads(q_rope, MLA_ROPE, lead=MLA_NOPE),
        _pad_heads(_rot_half_cols(q_rope, MLA_ROPE), MLA_ROPE, lead=MLA_NOPE)], axis=1).astype(BF16)

    wkv3 = w_ukv[l].reshape(MLA_KV_RANK, MLA_HEADS, MLA_NOPE + MLA_V)
    wkv = jnp.concatenate([
        _pad_heads(wkv3[..., :MLA_NOPE].reshape(MLA_KV_RANK, -1), MLA_NOPE),
        _pad_heads(wkv3[..., MLA_NOPE:].reshape(MLA_KV_RANK, -1), MLA_V)], axis=1).astype(BF16)

    def blockdiag(w):
        n, bi, bj = w.shape
        eye = jnp.eye(n, dtype=w.dtype)
        return (w[:, :, None, :] * eye[:, None, :, None]).reshape(n * bi, n * bj)

    w_gate = [jnp.concatenate([blockdiag(lru_wa[l, d]), blockdiag(lru_wx[l, d])], axis=1).astype(BF16) for d in range(2)]
    b_gate = [jnp.concatenate([lru_ba[l, d].reshape(1, -1), lru_bx[l, d].reshape(1, -1)], axis=1) for d in range(2)]

    g = grp_norm[l]
    wo = w_out[l]
    ga, gb = MLA_HEADS * MLA_V, MLA_HEADS * MLA_V + LRU_WIDTH

    def pad_rows(w, d):
        return jnp.pad(w.reshape(-1, d, w.shape[-1]), ((0, 0), (0, HP - d), (0, 0))).reshape(-1, w.shape[-1])

    out_w = dict(
        g_a=_pad_heads(g[None, :ga], MLA_V), g_b=g[None, ga:gb], g_c=_pad_heads(g[None, gb:], SWA_HEAD_DIM),
        w_a=pad_rows(wo[:ga], MLA_V).astype(BF16), w_b=wo[ga:gb].astype(BF16),
        w_c=pad_rows(wo[gb:], SWA_HEAD_DIM).astype(BF16))
    return w_in_p, wq, wkv, w_gate, b_gate, out_w


def _rope_table_kernel(pos_ref, inv_ref, cos_ref, sin_ref):
    ang = pos_ref[...] * inv_ref[...]
    cos_ref[...] = jnp.cos(ang)
    sin_ref[...] = jnp.sin(ang)


def _rope_tables(pos, inv, tm=512):
    t, w = pos.shape[0], inv.shape[1]
    return pl.pallas_call(
        _rope_table_kernel, name="rope_tables",
        grid=(t // tm,),
        in_specs=[pl.BlockSpec((tm, 1), lambda i: (i, 0)), pl.BlockSpec((1, w), lambda i: (0, 0))],
        out_specs=[pl.BlockSpec((tm, w), lambda i: (i, 0))] * 2,
        out_shape=[jax.ShapeDtypeStruct((t, w), F32)] * 2,
        compiler_params=_cparams("parallel"),
    )(pos, inv)


def _in_proj_kernel(x_ref, nmix_ref, win_ref, qn_ref, wq_ref, kvn_ref, wkv_ref, cosq_ref, sinq_ref, coss_ref, sins_ref,
                    q_out, k_out, v_out, sq_out, sk_out, sv_out, lx_out, lg_out):
    xn = _rmsnorm(x_ref[...], nmix_ref[...], D_MODEL)
    proj = jnp.dot(xn.astype(BF16), win_ref[...], preferred_element_type=F32)
    cosq, sinq, coss, sins = cosq_ref[...], sinq_ref[...], coss_ref[...], sins_ref[...]

    swa_scale = SWA_HEAD_DIM ** -0.5
    sq_out[...] = ((proj[:, _S_Q:_S_QR] * coss + proj[:, _S_QR:_S_K] * sins) * swa_scale).astype(BF16)
    sk_out[...] = (proj[:, _S_K:_S_KR] * coss[:, :SWA_KV] + proj[:, _S_KR:_S_V] * sins[:, :SWA_KV]).astype(BF16)
    sv_out[...] = proj[:, _S_V:_L_X].astype(BF16)
    lx_out[...] = proj[:, _L_X:_L_G]
    lg_out[...] = proj[:, _L_G:_K_R]

    cqn = _rmsnorm(proj[:, _C_Q:_C_KV], qn_ref[...], MLA_Q_RANK).astype(BF16)
    qq = jnp.dot(cqn, wq_ref[...], preferred_element_type=F32)
    mla_scale = (MLA_NOPE + MLA_ROPE) ** -0.5
    q_out[...] = ((qq[:, :MLA_QK] * cosq + qq[:, MLA_QK:] * sinq) * mla_scale).astype(BF16)

    kvn = _rmsnorm(proj[:, _C_KV:_S_Q], kvn_ref[...], MLA_KV_RANK).astype(BF16)
    kv = jnp.dot(kvn, wkv_ref[...], preferred_element_type=F32)
    kr = proj[:, _K_R:_IN_COLS]
    lane = lax.broadcasted_iota(I32, kr.shape, 1)
    in_rope = (lane >= MLA_NOPE) & (lane < MLA_NOPE + MLA_ROPE)
    kr_rot = jnp.where(in_rope, kr * cosq[:, :HP] + pltpu.roll(kr, HP - MLA_ROPE, axis=1) * sinq[:, :HP], 0.0)
    for h in range(MLA_HEADS):
        k_out[:, h * HP:(h + 1) * HP] = (kv[:, h * HP:(h + 1) * HP] + kr_rot).astype(BF16)
    v_out[...] = kv[:, MLA_QK:].astype(BF16)


def _in_proj(x, nmix, w_in_p, qn, wq, kvn, wkv, cosq, sinq, coss, sins, tm=256):
    t = x.shape[0]
    row = lambda w: pl.BlockSpec((tm, w), lambda i: (i, 0))
    full = lambda a: pl.BlockSpec(a.shape, lambda i: (0,) * a.ndim)
    widths = (MLA_QK, MLA_QK, MLA_QK, SWA_Q, SWA_KV, SWA_KV, LRU_WIDTH, LRU_WIDTH)
    dtypes = (BF16, BF16, BF16, BF16, BF16, BF16, F32, F32)
    return pl.pallas_call(
        _in_proj_kernel, name="in_proj",
        grid=(t // tm,),
        in_specs=[row(D_MODEL), full(nmix), full(w_in_p), full(qn), full(wq), full(kvn), full(wkv),
                  row(MLA_QK), row(MLA_QK), row(SWA_Q), row(SWA_Q)],
        out_specs=[row(w) for w in widths],
        out_shape=[jax.ShapeDtypeStruct((t, w), d) for w, d in zip(widths, dtypes)],
        compiler_params=_cparams("parallel"),
    )(x, nmix, w_in_p, qn, wq, kvn, wkv, cosq, sinq, coss, sins)


def _mla_kernel(q_ref, k_ref, v_ref, o_ref, *, tk):
    q = q_ref[...]
    tq = q.shape[0]
    nk = k_ref.shape[0] // tk

    def body(j, carry):
        m, l, acc = carry
        off = pl.multiple_of(j * tk, tk)
        k = k_ref[pl.ds(off, tk), :]
        v = v_ref[pl.ds(off, tk), :]
        s = lax.dot_general(q, k, (((1,), (1,)), ((), ())), preferred_element_type=F32)
        m_new = jnp.maximum(m, jnp.max(s, axis=-1, keepdims=True))
        alpha = jnp.exp(m - m_new)
        p = jnp.exp(s - m_new)
        l = alpha * l + jnp.sum(p, axis=-1, keepdims=True)
        acc = alpha * acc + jnp.dot(p.astype(BF16), v, preferred_element_type=F32)
        return m_new, l, acc

    init = (jnp.full((tq, 1), -jnp.inf, F32), jnp.zeros((tq, 1), F32), jnp.zeros((tq, HP), F32))
    _, l, acc = lax.fori_loop(0, nk, body, init)
    o_ref[...] = acc / l


def _mla_attention(q, k, v, batch, seq, tq=512, tk=512):
    nq = seq // tq
    return pl.pallas_call(
        functools.partial(_mla_kernel, tk=tk), name="mla_attention",
        grid=(batch, MLA_HEADS, nq),
        in_specs=[pl.BlockSpec((tq, HP), lambda b, h, i: (b * nq + i, h)),
                  pl.BlockSpec((seq, HP), lambda b, h, i: (b, h)),
                  pl.BlockSpec((seq, HP), lambda b, h, i: (b, h))],
        out_specs=pl.BlockSpec((tq, HP), lambda b, h, i: (b * nq + i, h)),
        out_shape=jax.ShapeDtypeStruct((batch * seq, MLA_QK), F32),
        compiler_params=_cparams("parallel", "parallel", "arbitrary"),
    )(q, k, v)


def _swa_kernel(sink_ref, q_ref, kp_ref, kc_ref, kn_ref, vp_ref, vc_ref, vn_ref, o_ref, *, seq):
    w = SWA_WINDOW
    tq = q_ref.shape[0]
    i = pl.program_id(1)
    kcat = jnp.concatenate([kp_ref[...], kc_ref[...], kn_ref[...]], axis=0)
    vcat = jnp.concatenate([vp_ref[...], vc_ref[...], vn_ref[...]], axis=0)
    for j in range(tq // w):
        qpos = i * tq + j * w + lax.broadcasted_iota(I32, (w, 3 * w), 0)
        kpos = i * tq + (j - 1) * w + lax.broadcasted_iota(I32, (w, 3 * w), 1)
        valid = (jnp.abs(kpos - qpos) <= w) & (kpos >= 0) & (kpos < seq)
        for h in range(SWA_Q_HEADS):
            kh = h // (SWA_Q_HEADS // SWA_KV_HEADS)
            qh = q_ref[j * w:(j + 1) * w, h * HP:(h + 1) * HP]
            kj = kcat[j * w:(j + 3) * w, kh * HP:(kh + 1) * HP]
            vj = vcat[j * w:(j + 3) * w, kh * HP:(kh + 1) * HP]
            s = lax.dot_general(qh, kj, (((1,), (1,)), ((), ())), preferred_element_type=F32)
            s = jnp.where(valid, s, NEG_BIG)
            sink = sink_ref[0, h]
            m = jnp.maximum(jnp.max(s, axis=-1, keepdims=True), sink)
            p = jnp.exp(s - m)
            den = jnp.sum(p, axis=-1, keepdims=True) + jnp.exp(sink - m)
            o_ref[j * w:(j + 1) * w, h * HP:(h + 1) * HP] = jnp.dot(p.astype(BF16), vj, preferred_element_type=F32) / den


def _swa_attention(sq, sk, sv, sink, batch, seq, tq=512):
    w = SWA_WINDOW
    nq, nw, r = seq // tq, seq // w, tq // w
    prev = pl.BlockSpec((w, SWA_KV), lambda b, i: (b * nw + jnp.maximum(i * r - 1, 0), 0))
    cur = pl.BlockSpec((tq, SWA_KV), lambda b, i: (b * nq + i, 0))
    nxt = pl.BlockSpec((w, SWA_KV), lambda b, i: (b * nw + jnp.minimum((i + 1) * r, nw - 1), 0))
    return pl.pallas_call(
        functools.partial(_swa_kernel, seq=seq), name="swa_attention",
        grid=(batch, nq),
        in_specs=[pl.BlockSpec(memory_space=pltpu.SMEM),
                  pl.BlockSpec((tq, SWA_Q), lambda b, i: (b * nq + i, 0)),
                  prev, cur, nxt, prev, cur, nxt],
        out_specs=pl.BlockSpec((tq, SWA_Q), lambda b, i: (b * nq + i, 0)),
        out_shape=jax.ShapeDtypeStruct((batch * seq, SWA_Q), F32),
        compiler_params=_cparams("parallel", "parallel"),
    )(sink, sq, sk, sk, sk, sv, sv, sv)


def _lru_kernel(xpf_ref, xcf_ref, xnf_ref, xpb_ref, xcb_ref, xnb_ref, cw_ref, cb_ref, wf_ref, bf_ref, wb_ref, bb_ref,
                lam_ref, hf_out, hb_out, carry_f, carry_b, *, nt):
    tm = xcf_ref.shape[0]
    i = pl.program_id(1)
    halo = xpf_ref.shape[0]

    @pl.when(i == 0)
    def _():
        carry_f[...] = jnp.zeros_like(carry_f)
        carry_b[...] = jnp.zeros_like(carry_b)

    def gates(xp_ref, xc_ref, xn_ref, first, last, w_ref, b_ref, lam):
        prev = jnp.where(first, 0.0, xp_ref[...])
        nxt = jnp.where(last, 0.0, xn_ref[...])
        xcat = jnp.concatenate([prev, xc_ref[...], nxt], axis=0)
        cw = cw_ref[...]
        conv = cb_ref[...]
        for tap in range(cw.shape[0]):
            conv = conv + cw[tap:tap + 1, :] * xcat[halo - 1 + tap: halo - 1 + tap + tm, :]
        g = jnp.dot(conv.astype(BF16), w_ref[...], preferred_element_type=F32) + b_ref[...]
        r = _sigmoid(g[:, :LRU_WIDTH])
        gate_i = _sigmoid(g[:, LRU_WIDTH:])
        softplus = jnp.maximum(-lam, 0.0) + jnp.log1p(jnp.exp(-jnp.abs(lam)))
        log_a = -LRU_C * r * softplus
        a = jnp.exp(log_a)
        b = jnp.sqrt(1.0 - a * a) * (gate_i * conv)
        return a, b

    row = lax.broadcasted_iota(I32, (tm, LRU_WIDTH), 0)

    def scan(a, b, reverse):
        k = 1
        while k < tm:
            if reverse:
                keep = row < tm - k
                shift = tm - k
            else:
                keep = row >= k
                shift = k
            a_s = jnp.where(keep, pltpu.roll(a, shift, axis=0), 1.0)
            b_s = jnp.where(keep, pltpu.roll(b, shift, axis=0), 0.0)
            b = a * b_s + b
            a = a * a_s
            k *= 2
        return a, b

    a, b = gates(xpf_ref, xcf_ref, xnf_ref, i == 0, i == nt - 1, wf_ref, bf_ref, lam_ref[0:1, :])
    a, b = scan(a, b, False)
    h = a * carry_f[...] + b
    hf_out[...] = h
    carry_f[...] = h[tm - 1:tm, :]

    a, b = gates(xpb_ref, xcb_ref, xnb_ref, i == nt - 1, i == 0, wb_ref, bb_ref, lam_ref[1:2, :])
    a, b = scan(a, b, True)
    h = a * carry_b[...] + b
    hb_out[...] = h
    carry_b[...] = h[0:1, :]


def _lru_scan(lx, conv_w, conv_b, w_gate, b_gate, lam, batch, seq, tm=256):
    nt, hb = seq // tm, seq // SUBLANES
    r = tm // SUBLANES
    fwd = lambda b, i: i
    bwd = lambda b, i: nt - 1 - i

    def specs(tile):
        return [pl.BlockSpec((SUBLANES, LRU_WIDTH), lambda b, i: (b * hb + jnp.maximum(tile(b, i) * r - 1, 0), 0)),
                pl.BlockSpec((tm, LRU_WIDTH), lambda b, i: (b * nt + tile(b, i), 0)),
                pl.BlockSpec((SUBLANES, LRU_WIDTH), lambda b, i: (b * hb + jnp.minimum((tile(b, i) + 1) * r, hb - 1), 0))]

    full = lambda a: pl.BlockSpec(a.shape, lambda b, i: (0,) * a.ndim)
    return pl.pallas_call(
        functools.partial(_lru_kernel, nt=nt), name="lru_scan",
        grid=(batch, nt),
        in_specs=specs(fwd) + specs(bwd) + [full(conv_w), full(conv_b), full(w_gate[0]), full(b_gate[0]),
                                            full(w_gate[1]), full(b_gate[1]), full(lam)],
        out_specs=[pl.BlockSpec((tm, LRU_WIDTH), lambda b, i: (b * nt + i, 0)),
                   pl.BlockSpec((tm, LRU_WIDTH), lambda b, i: (b * nt + nt - 1 - i, 0))],
        out_shape=[jax.ShapeDtypeStruct((batch * seq, LRU_WIDTH), F32)] * 2,
        scratch_shapes=[pltpu.VMEM((1, LRU_WIDTH), F32), pltpu.VMEM((1, LRU_WIDTH), F32)],
        compiler_params=_cparams("parallel", "arbitrary"),
    )(lx, lx, lx, lx, lx, lx, conv_w, conv_b, w_gate[0], b_gate[0], w_gate[1], b_gate[1], lam)


def _out_proj_kernel(x_ref, oa_ref, hf_ref, hb_ref, lg_ref, oc_ref, ga_ref, gb_ref, gc_ref, wa_ref, wb_ref, wc_ref,
                     nffn_ref, wq_ref, x1_out, xn_out, q_out):
    mix_a = _rmsnorm(oa_ref[...], ga_ref[...], MLA_HEADS * MLA_V).astype(BF16)
    o_lru = (hf_ref[...] + hb_ref[...]) * _gelu(lg_ref[...])
    mix_b = _rmsnorm(o_lru, gb_ref[...], LRU_WIDTH).astype(BF16)
    mix_c = _rmsnorm(oc_ref[...], gc_ref[...], SWA_Q_HEADS * SWA_HEAD_DIM).astype(BF16)
    x1 = (x_ref[...]
          + jnp.dot(mix_a, wa_ref[...], preferred_element_type=F32)
          + jnp.dot(mix_b, wb_ref[...], preferred_element_type=F32)
          + jnp.dot(mix_c, wc_ref[...], preferred_element_type=F32))
    x1_out[...] = x1
    xn = _rmsnorm(x1, nffn_ref[...], D_MODEL)
    xn_out[...] = xn
    q_out[...] = jnp.dot(xn.astype(BF16), wq_ref[...], preferred_element_type=F32)


def _out_proj(x, o_mla, h_f, h_b, l_g, o_swa, ow, nffn, wq, tm=256):
    t = x.shape[0]
    row = lambda w: pl.BlockSpec((tm, w), lambda i: (i, 0))
    full = lambda a: pl.BlockSpec(a.shape, lambda i: (0,) * a.ndim)
    nq = wq.shape[1]
    return pl.pallas_call(
        _out_proj_kernel, name="out_proj",
        grid=(t // tm,),
        in_specs=[row(D_MODEL), row(MLA_QK), row(LRU_WIDTH), row(LRU_WIDTH), row(LRU_WIDTH), row(SWA_Q),
                  full(ow["g_a"]), full(ow["g_b"]), full(ow["g_c"]), full(ow["w_a"]), full(ow["w_b"]), full(ow["w_c"]),
                  full(nffn), full(wq)],
        out_specs=[row(D_MODEL), row(D_MODEL), row(nq)],
        out_shape=[jax.ShapeDtypeStruct((t, D_MODEL), F32), jax.ShapeDtypeStruct((t, D_MODEL), F32),
                   jax.ShapeDtypeStruct((t, nq), F32)],
        compiler_params=_cparams("parallel"),
    )(x, o_mla, h_f, h_b, l_g, o_swa, ow["g_a"], ow["g_b"], ow["g_c"], ow["w_a"], ow["w_b"], ow["w_c"], nffn, wq)


def _top_k_rows(s, k, payload=None):
    n = s.shape[0]
    row = lax.broadcasted_iota(I32, s.shape, 0)
    vals, picks = [], []
    for _ in range(k):
        m = jnp.max(s, axis=0, keepdims=True)
        idx = jnp.min(jnp.where(s == m, row, n), axis=0, keepdims=True)
        hit = row == idx
        vals.append(m)
        picks.append(idx if payload is None else jnp.sum(jnp.where(hit, payload, 0), axis=0, keepdims=True))
        s = jnp.where(hit, -jnp.inf, s)
    return jnp.concatenate(vals, axis=0), jnp.concatenate(picks, axis=0)


def _peer_topk_kernel(q_ref, keys_ref, id_out, gate_out):
    half = PEER_DKEY // 2
    dn = (((1,), (1,)), ((), ()))
    q = q_ref[...]
    s1 = lax.dot_general(keys_ref[0], q[:, :half], dn, preferred_element_type=F32, precision=lax.Precision.HIGHEST)
    s2 = lax.dot_general(keys_ref[1], q[:, half:], dn, preferred_element_type=F32, precision=lax.Precision.HIGHEST)
    v1, i1 = _top_k_rows(s1, PEER_TOPK)
    v2, i2 = _top_k_rows(s2, PEER_TOPK)
    cand = jnp.concatenate([v1[a:a + 1, :] + v2 for a in range(PEER_TOPK)], axis=0)
    cand_id = jnp.concatenate([i1[a:a + 1, :] * PEER_NKEYS + i2 for a in range(PEER_TOPK)], axis=0)
    sc, ids = _top_k_rows(cand, PEER_TOPK, payload=cand_id)
    e = jnp.exp(sc - sc[0:1, :])
    id_out[0] = ids
    gate_out[0] = e / jnp.sum(e, axis=0, keepdims=True)


def _peer_topk(q, subkeys, tm=256):
    t = q.shape[0]
    out = pl.BlockSpec((1, PEER_TOPK, tm), lambda i, h: (h, 0, i))
    return pl.pallas_call(
        _peer_topk_kernel, name="peer_topk",
        grid=(t // tm, PEER_HEADS),
        in_specs=[pl.BlockSpec((tm, PEER_DKEY), lambda i, h: (i, h)),
                  pl.BlockSpec(subkeys.shape, lambda i, h: (0, 0, 0))],
        out_specs=[out, out],
        out_shape=[jax.ShapeDtypeStruct((PEER_HEADS, PEER_TOPK, t), I32),
                   jax.ShapeDtypeStruct((PEER_HEADS, PEER_TOPK, t), F32)],
        compiler_params=_cparams("parallel", "parallel"),
    )(q, subkeys)


def _fold_sublanes(vals, masks):
    for level, mask in enumerate(masks):
        nxt = []
        for a, b in zip(vals[0::2], vals[1::2]):
            nxt.append(jnp.where(mask, a, b) + pltpu.roll(jnp.where(mask, b, a), 1 << level, axis=0))
        vals = nxt
    (out,) = vals
    return out


def _pack_table(w):
    bits = lax.bitcast_convert_type(w.astype(BF16), jnp.uint16).astype(jnp.uint32)
    half = D_MODEL // 2
    words = lax.bitcast_convert_type((bits[:, half:] << 16) | bits[:, :half], I32)
    return jnp.pad(words.reshape(-1, LANES), ((0, SUBLANES), (0, 0)))


def _unpack(w):
    return pltpu.bitcast(w << 16, F32), pltpu.bitcast(w & -0x10000, F32)


def _peer_hidden_kernel(id_ref, x_ref, u_ref, h_out, r_scr):
    tm = x_ref.shape[0]
    sub = lax.broadcasted_iota(I32, (SUBLANES, LANES), 0)
    sub_masks = [(sub % (2 << lv)) < (1 << lv) for lv in range(3)]
    groups = PEER_SLOTS // SUBLANES
    ones = jnp.ones((SUBLANES, LANES), BF16)
    dn = (((1,), (1,)), ((), ()))

    def token(tt, g):
        t = g * SUBLANES + tt
        xt = x_ref[t]
        x_lo = jnp.where(sub < PACK_ROWS, xt, 0.0)
        x_hi = jnp.where(sub < PACK_ROWS, pltpu.roll(xt, PACK_ROWS, axis=0), 0.0)
        for j in range(groups):
            prods = []
            for s in range(SUBLANES):
                row = pl.multiple_of(id_ref[t * PEER_SLOTS + j * SUBLANES + s], PACK_ROWS)
                lo, hi = _unpack(u_ref[pl.ds(row, SUBLANES), :])
                prods.append(lo * x_lo + hi * x_hi)
            r_scr[pl.ds(pl.multiple_of(tt * PEER_SLOTS + j * SUBLANES, SUBLANES), SUBLANES), :] = _fold_sublanes(prods, sub_masks)
        return g

    def group(g, _):
        lax.fori_loop(0, SUBLANES, token, g)
        r = r_scr[...]
        hi = r.astype(BF16)
        lo = (r - hi.astype(F32)).astype(BF16)
        sums = (lax.dot_general(ones, hi, dn, preferred_element_type=F32)
                + lax.dot_general(ones, lo, dn, preferred_element_type=F32))
        out = jnp.zeros((SUBLANES, LANES), F32)
        for tt in range(SUBLANES):
            out = jnp.where(sub == tt, sums[:, tt * PEER_SLOTS:(tt + 1) * PEER_SLOTS], out)
        h_out[pl.ds(pl.multiple_of(g * SUBLANES, SUBLANES), SUBLANES), :] = out
        return 0

    lax.fori_loop(0, tm // SUBLANES, group, 0)


def _smem_rows(tm):
    return pl.BlockSpec((tm * PEER_SLOTS,), lambda i: (i,), memory_space=pltpu.SMEM)


def _table_spec():
    return pl.BlockSpec((TABLE_ROWS, LANES), lambda i: (0, 0), pipeline_mode=pl.Buffered(1))


def _peer_hidden(rows, x3, u_packed, tm=128):
    t = x3.shape[0]
    return pl.pallas_call(
        _peer_hidden_kernel, name="peer_hidden",
        grid=(t // tm,),
        in_specs=[_smem_rows(tm), pl.BlockSpec((tm, SUBLANES, LANES), lambda i: (i, 0, 0)), _table_spec()],
        out_specs=pl.BlockSpec((tm, PEER_SLOTS), lambda i: (i, 0)),
        out_shape=jax.ShapeDtypeStruct((t, PEER_SLOTS), F32),
        scratch_shapes=[pltpu.VMEM((SUBLANES * PEER_SLOTS, LANES), F32)],
        compiler_params=_cparams("arbitrary"),
    )(rows, x3, u_packed)


def _peer_coef_kernel(h_ref, gate_ref, c_out):
    ct = (gate_ref[...] * _gelu(h_ref[...])).T
    for g in range(ct.shape[1] // SUBLANES):
        c_out[g] = ct if g == 0 else pltpu.roll(ct, LANES - SUBLANES * g, axis=1)


def _peer_coef(hid, gate, tm=LANES):
    t = gate.shape[0]
    return pl.pallas_call(
        _peer_coef_kernel, name="peer_coef",
        grid=(t // tm,),
        in_specs=[pl.BlockSpec((tm, PEER_SLOTS), lambda i: (i, 0)), pl.BlockSpec((tm, PEER_SLOTS), lambda i: (i, 0))],
        out_specs=pl.BlockSpec((tm // SUBLANES, PEER_SLOTS, LANES), lambda i: (i, 0, 0)),
        out_shape=jax.ShapeDtypeStruct((t // SUBLANES, PEER_SLOTS, LANES), F32),
        compiler_params=_cparams("parallel"),
    )(hid, gate)


def _peer_value_kernel(id_ref, c_ref, v_ref, y_out, cb_scr):
    tm = y_out.shape[0]
    sub = lax.broadcasted_iota(I32, (SUBLANES, LANES), 0)

    def group(g, _):
        cg = c_ref[g]
        for tt in range(SUBLANES):
            cb_scr[tt * PEER_SLOTS:(tt + 1) * PEER_SLOTS, :] = jnp.broadcast_to(cg[:, tt:tt + 1], (PEER_SLOTS, LANES))

        def token(tt, _):
            t = g * SUBLANES + tt

            acc_lo = jnp.zeros((SUBLANES, LANES), F32)
            acc_hi = jnp.zeros((SUBLANES, LANES), F32)
            for j in range(PEER_SLOTS // SUBLANES):
                t_lo, t_hi = [], []
                for s in range(SUBLANES):
                    e = j * SUBLANES + s
                    row = pl.multiple_of(id_ref[t * PEER_SLOTS + e], PACK_ROWS)
                    c = jnp.broadcast_to(cb_scr[pl.ds(tt * PEER_SLOTS + e, 1), :], (SUBLANES, LANES))
                    lo, hi = _unpack(v_ref[pl.ds(row, SUBLANES), :])
                    t_lo.append(c * lo)
                    t_hi.append(c * hi)
                while len(t_lo) > 1:
                    t_lo = [a + b for a, b in zip(t_lo[0::2], t_lo[1::2])]
                    t_hi = [a + b for a, b in zip(t_hi[0::2], t_hi[1::2])]
                acc_lo = acc_lo + t_lo[0]
                acc_hi = acc_hi + t_hi[0]
            y_out[t] = jnp.where(sub < PACK_ROWS, acc_lo, pltpu.roll(acc_hi, PACK_ROWS, axis=0))
            return 0

        lax.fori_loop(0, SUBLANES, token, 0)
        return 0

    lax.fori_loop(0, tm // SUBLANES, group, 0)


def _peer_value(rows, coef, v_packed, tm=128):
    t = coef.shape[0] * SUBLANES
    return pl.pallas_call(
        _peer_value_kernel, name="peer_value",
        grid=(t // tm,),
        in_specs=[_smem_rows(tm), pl.BlockSpec((tm // SUBLANES, PEER_SLOTS, LANES), lambda i: (i, 0, 0)), _table_spec()],
        out_specs=pl.BlockSpec((tm, SUBLANES, LANES), lambda i: (i, 0, 0)),
        out_shape=jax.ShapeDtypeStruct((t, SUBLANES, LANES), F32),
        scratch_shapes=[pltpu.VMEM((SUBLANES * PEER_SLOTS, LANES), F32)],
        compiler_params=_cparams("arbitrary"),
    )(rows, coef, v_packed)


def _residual_kernel(x_ref, y_ref, g_ref, o_ref, *, final):
    x = x_ref[...] + y_ref[...]
    o_ref[...] = _rmsnorm(x, g_ref[...], D_MODEL) if final else x


def _residual(x1, y, gain, final, tm=512):
    t = x1.shape[0]
    return pl.pallas_call(
        functools.partial(_residual_kernel, final=final), name="peer_residual",
        grid=(t // tm,),
        in_specs=[pl.BlockSpec((tm, D_MODEL), lambda i: (i, 0)),
                  pl.BlockSpec((tm, D_MODEL), lambda i: (i, 0)),
                  pl.BlockSpec((1, D_MODEL), lambda i: (0, 0))],
        out_specs=pl.BlockSpec((tm, D_MODEL), lambda i: (i, 0)),
        out_shape=jax.ShapeDtypeStruct((t, D_MODEL), F32),
        compiler_params=_cparams("parallel"),
    )(x1, y, gain)


def _peer_ffn(x1, xn, q, subkeys, u, v, gain, final):
    t = x1.shape[0]
    ids_h, gate_h = _peer_topk(q, subkeys)
    ids = ids_h.reshape(PEER_SLOTS, t).T
    gate = gate_h.reshape(PEER_SLOTS, t).T
    rows = (ids * PACK_ROWS).reshape(-1)
    hid = _peer_hidden(rows, xn.reshape(t, SUBLANES, LANES), _pack_table(u))
    coef = _peer_coef(hid, gate)
    y = _peer_value(rows, coef, _pack_table(v))
    return _residual(x1, y.reshape(t, D_MODEL), gain, final)


def kernel(x, positions, norm_mix, w_in, q_norm, w_uq, kv_norm, w_ukv, conv_w, conv_b, lru_wa, lru_ba, lru_wx, lru_bx,
           lru_lambda, swa_sink, grp_norm, w_out, norm_ffn, peer_wq, peer_subkeys, peer_u, peer_v, norm_final):
    batch, seq, _ = x.shape
    t = batch * seq
    depth = w_in.shape[0]
    xt = x.reshape(t, D_MODEL)

    pos = positions.astype(F32).reshape(t, 1)
    zeros = lambda n: jnp.zeros((n,), F32)
    inv_m, inv_s = _inv_freq(MLA_ROPE), _inv_freq(SWA_HEAD_DIM)
    inv_q = jnp.tile(jnp.concatenate([zeros(MLA_NOPE), inv_m, inv_m, zeros(HP - MLA_NOPE - MLA_ROPE)]), MLA_HEADS)
    inv_w = jnp.tile(jnp.concatenate([inv_s, inv_s, zeros(HP - SWA_HEAD_DIM)]), SWA_Q_HEADS)
    cosq, sinq = _rope_tables(pos, inv_q[None, :])
    coss, sins = _rope_tables(pos, inv_w[None, :])

    for l in range(depth):
        w_in_p, wq, wkv, w_gate, b_gate, ow = _layer_weights(
            l, w_in, w_uq, w_ukv, lru_wa, lru_ba, lru_wx, lru_bx, grp_norm, w_out)
        q, k, v, sq, sk, sv, lx, lg = _in_proj(
            xt, norm_mix[l][None, :], w_in_p, q_norm[l][None, :], wq, kv_norm[l][None, :], wkv, cosq, sinq, coss, sins)
        o_mla = _mla_attention(q, k, v, batch, seq)
        o_swa = _swa_attention(sq, sk, sv, swa_sink[l][None, :], batch, seq)
        h_f, h_b = _lru_scan(lx, conv_w[l], conv_b[l][None, :], w_gate, b_gate, lru_lambda[l], batch, seq)
        x1, xn, pq = _out_proj(xt, o_mla, h_f, h_b, lg, o_swa, ow, norm_ffn[l][None, :], peer_wq[l].astype(BF16))
        final = l == depth - 1
        gain = norm_final[None, :] if final else jnp.ones((1, D_MODEL), F32)
        xt = _peer_ffn(x1, xn, pq, peer_subkeys[l], peer_u[l], peer_v[l], gain, final)
    return xt.reshape(batch, seq, D_MODEL)
```

```python
import functools

import jax
import jax.numpy as jnp
from jax import lax
from jax.experimental import pallas as pl
from jax.experimental.pallas import tpu as pltpu

F32 = jnp.float32
BF16 = jnp.bfloat16
I32 = jnp.int32

D_MODEL = 1024
MLA_HEADS, MLA_NOPE, MLA_ROPE, MLA_V = 6, 64, 32, 64
MLA_Q_RANK, MLA_KV_RANK = 256, 128
LRU_WIDTH, LRU_BLOCKS, LRU_C = 384, 6, 8.0
SWA_Q_HEADS, SWA_KV_HEADS, SWA_HEAD_DIM, SWA_WINDOW = 4, 2, 64, 128
ROPE_THETA, EPS, NEG_BIG = 10000.0, 1e-6, -1e30
PEER_HEADS, PEER_NKEYS, PEER_DKEY, PEER_TOPK = 8, 128, 256, 16
PEER_SLOTS = PEER_HEADS * PEER_TOPK
PEER_EXPERTS = PEER_NKEYS * PEER_NKEYS

LANES = 128
SUBLANES = 8
HP = LANES
MLA_QK = MLA_HEADS * HP
SWA_Q = SWA_Q_HEADS * HP
SWA_KV = SWA_KV_HEADS * HP
VMEM_LIMIT = 56 * 1024 * 1024
PACK_ROWS = D_MODEL // (2 * LANES)
TABLE_ROWS = PEER_EXPERTS * PACK_ROWS + SUBLANES

_C_Q, _C_KV, _S_Q, _S_QR, _S_K, _S_KR, _S_V, _L_X, _L_G, _K_R, _IN_COLS = (
    0, 256, 384, 896, 1408, 1664, 1920, 2176, 2560, 2944, 3072)


def _cparams(*sem):
    return pltpu.CompilerParams(dimension_semantics=sem, vmem_limit_bytes=VMEM_LIMIT)


def _rmsnorm(x, g, n):
    return x * lax.rsqrt(jnp.sum(x * x, axis=-1, keepdims=True) * (1.0 / n) + EPS) * g


def _sigmoid(x):
    return 1.0 / (1.0 + jnp.exp(-x))


def _gelu(x):
    return 0.5 * x * (1.0 + lax.erf(x * (2.0 ** -0.5)))


def _rot_half_cols(w, d):
    k = w.shape[0]
    w3 = w.reshape(k, -1, d)
    return jnp.concatenate([-w3[..., d // 2:], w3[..., : d // 2]], axis=-1).reshape(k, -1)


def _pad_heads(w, d, lead=0):
    k = w.shape[0]
    w3 = w.reshape(k, -1, d)
    return jnp.pad(w3, ((0, 0), (0, 0), (lead, HP - d - lead))).reshape(k, -1)


def _inv_freq(d):
    return ROPE_THETA ** (-jnp.arange(0, d, 2, dtype=F32) / d)


def _layer_weights(l, w_in, w_uq, w_ukv, lru_wa, lru_ba, lru_wx, lru_bx, grp_norm, w_out):
    wi = w_in[l]
    c_q, c_kv, k_r = wi[:, 0:256], wi[:, 256:384], wi[:, 384:416]
    s_q, s_k, s_v = wi[:, 416:672], wi[:, 672:800], wi[:, 800:928]
    l_x, l_g = wi[:, 928:1312], wi[:, 1312:1696]
    k_r_blk = jnp.concatenate([jnp.zeros((D_MODEL, MLA_NOPE), F32), k_r, _rot_half_cols(k_r, MLA_ROPE)], axis=1)
    w_in_p = jnp.concatenate([
        c_q, c_kv,
        _pad_heads(s_q, SWA_HEAD_DIM), _pad_heads(_rot_half_cols(s_q, SWA_HEAD_DIM), SWA_HEAD_DIM),
        _pad_heads(s_k, SWA_HEAD_DIM), _pad_heads(_rot_half_cols(s_k, SWA_HEAD_DIM), SWA_HEAD_DIM),
        _pad_heads(s_v, SWA_HEAD_DIM), l_x, l_g, k_r_blk], axis=1).astype(BF16)

    wq3 = w_uq[l].reshape(MLA_Q_RANK, MLA_HEADS, MLA_NOPE + MLA_ROPE)
    q_nope = wq3[..., :MLA_NOPE].reshape(MLA_Q_RANK, -1)
    q_rope = wq3[..., MLA_NOPE:].reshape(MLA_Q_RANK, -1)
    wq = jnp.concatenate([
        _pad_heads(q_nope, MLA_NOPE) + _pad_heads(q_rope, MLA_ROPE, lead=MLA_NOPE),
        _pad_heads(_rot_half_cols(q_rope, MLA_ROPE), MLA_ROPE, lead=MLA_NOPE)], axis=1).astype(BF16)

    wkv3 = w_ukv[l].reshape(MLA_KV_RANK, MLA_HEADS, MLA_NOPE + MLA_V)
    wkv = jnp.concatenate([
        _pad_heads(wkv3[..., :MLA_NOPE].reshape(MLA_KV_RANK, -1), MLA_NOPE),
        _pad_heads(wkv3[..., MLA_NOPE:].reshape(MLA_KV_RANK, -1), MLA_V)], axis=1).astype(BF16)

    def blockdiag(w):
        n, bi, bj = w.shape
        eye = jnp.eye(n, dtype=w.dtype)
        return (w[:, :, None, :] * eye[:, None, :, None]).reshape(n * bi, n * bj)

    w_gate = [jnp.concatenate([blockdiag(lru_wa[l, d]), blockdiag(lru_wx[l, d])], axis=1).astype(BF16) for d in range(2)]
    b_gate = [jnp.concatenate([lru_ba[l, d].reshape(1, -1), lru_bx[l, d].reshape(1, -1)], axis=1) for d in range(2)]

    g = grp_norm[l]
    wo = w_out[l]
    ga, gb = MLA_HEADS * MLA_V, MLA_HEADS * MLA_V + LRU_WIDTH

    def pad_rows(w, d):
        return jnp.pad(w.reshape(-1, d, w.shape[-1]), ((0, 0), (0, HP - d), (0, 0))).reshape(-1, w.shape[-1])

    out_w = dict(
        g_a=_pad_heads(g[None, :ga], MLA_V), g_b=g[None, ga:gb], g_c=_pad_heads(g[None, gb:], SWA_HEAD_DIM),
        w_a=pad_rows(wo[:ga], MLA_V).astype(BF16), w_b=wo[ga:gb].astype(BF16),
        w_c=pad_rows(wo[gb:], SWA_HEAD_DIM).astype(BF16))
    return w_in_p, wq, wkv, w_gate, b_gate, out_w


def _rope_table_kernel(pos_ref, inv_ref, cos_ref, sin_ref):
    ang = pos_ref[...] * inv_ref[...]
    cos_ref[...] = jnp.cos(ang)
    sin_ref[...] = jnp.sin(ang)


def _rope_tables(pos, inv, tm=512):
    t, w = pos.shape[0], inv.shape[1]
    return pl.pallas_call(
        _rope_table_kernel, name="rope_tables",
        grid=(t // tm,),
        in_specs=[pl.BlockSpec((tm, 1), lambda i: (i, 0)), pl.BlockSpec((1, w), lambda i: (0, 0))],
        out_specs=[pl.BlockSpec((tm, w), lambda i: (i, 0))] * 2,
        out_shape=[jax.ShapeDtypeStruct((t, w), F32)] * 2,
        compiler_params=_cparams("parallel"),
    )(pos, inv)


def _in_proj_kernel(x_ref, nmix_ref, win_ref, qn_ref, wq_ref, kvn_ref, wkv_ref, cosq_ref, sinq_ref, coss_ref, sins_ref,
                    q_out, k_out, v_out, sq_out, sk_out, sv_out, lx_out, lg_out):
    xn = _rmsnorm(x_ref[...], nmix_ref[...], D_MODEL)
    proj = jnp.dot(xn.astype(BF16), win_ref[...], preferred_element_type=F32)
    cosq, sinq, coss, sins = cosq_ref[...], sinq_ref[...], coss_ref[...], sins_ref[...]

    swa_scale = SWA_HEAD_DIM ** -0.5
    sq_out[...] = ((proj[:, _S_Q:_S_QR] * coss + proj[:, _S_QR:_S_K] * sins) * swa_scale).astype(BF16)
    sk_out[...] = (proj[:, _S_K:_S_KR] * coss[:, :SWA_KV] + proj[:, _S_KR:_S_V] * sins[:, :SWA_KV]).astype(BF16)
    sv_out[...] = proj[:, _S_V:_L_X].astype(BF16)
    lx_out[...] = proj[:, _L_X:_L_G]
    lg_out[...] = proj[:, _L_G:_K_R]

    cqn = _rmsnorm(proj[:, _C_Q:_C_KV], qn_ref[...], MLA_Q_RANK).astype(BF16)
    qq = jnp.dot(cqn, wq_ref[...], preferred_element_type=F32)
    mla_scale = (MLA_NOPE + MLA_ROPE) ** -0.5
    q_out[...] = ((qq[:, :MLA_QK] * cosq + qq[:, MLA_QK:] * sinq) * mla_scale).astype(BF16)

    kvn = _rmsnorm(proj[:, _C_KV:_S_Q], kvn_ref[...], MLA_KV_RANK).astype(BF16)
    kv = jnp.dot(kvn, wkv_ref[...], preferred_element_type=F32)
    kr = proj[:, _K_R:_IN_COLS]
    lane = lax.broadcasted_iota(I32, kr.shape, 1)
    in_rope = (lane >= MLA_NOPE) & (lane < MLA_NOPE + MLA_ROPE)
    kr_rot = jnp.where(in_rope, kr * cosq[:, :HP] + pltpu.roll(kr, HP - MLA_ROPE, axis=1) * sinq[:, :HP], 0.0)
    for h in range(MLA_HEADS):
        k_out[:, h * HP:(h + 1) * HP] = (kv[:, h * HP:(h + 1) * HP] + kr_rot).astype(BF16)
    vlane = lax.broadcasted_iota(I32, (kv.shape[0], MLA_QK), 1)
    v_out[...] = jnp.where(vlane % HP == MLA_V, 1.0, kv[:, MLA_QK:]).astype(BF16)


def _in_proj(x, nmix, w_in_p, qn, wq, kvn, wkv, cosq, sinq, coss, sins, tm=256):
    t = x.shape[0]
    row = lambda w: pl.BlockSpec((tm, w), lambda i: (i, 0))
    full = lambda a: pl.BlockSpec(a.shape, lambda i: (0,) * a.ndim)
    widths = (MLA_QK, MLA_QK, MLA_QK, SWA_Q, SWA_KV, SWA_KV, LRU_WIDTH, LRU_WIDTH)
    dtypes = (BF16, BF16, BF16, BF16, BF16, BF16, F32, F32)
    return pl.pallas_call(
        _in_proj_kernel, name="in_proj",
        grid=(t // tm,),
        in_specs=[row(D_MODEL), full(nmix), full(w_in_p), full(qn), full(wq), full(kvn), full(wkv),
                  row(MLA_QK), row(MLA_QK), row(SWA_Q), row(SWA_Q)],
        out_specs=[row(w) for w in widths],
        out_shape=[jax.ShapeDtypeStruct((t, w), d) for w, d in zip(widths, dtypes)],
        compiler_params=_cparams("parallel"),
    )(x, nmix, w_in_p, qn, wq, kvn, wkv, cosq, sinq, coss, sins)


def _mla_kernel(q_ref, k_ref, v_ref, o_ref, *, tk):
    q = q_ref[...]
    tq = q.shape[0]
    nk = k_ref.shape[0] // tk

    def body(j, carry):
        m, acc = carry
        off = pl.multiple_of(j * tk, tk)
        k = k_ref[pl.ds(off, tk), :]
        v = v_ref[pl.ds(off, tk), :]
        s = lax.dot_general(q, k, (((1,), (1,)), ((), ())), preferred_element_type=F32)
        m_new = jnp.maximum(m, jnp.max(s, axis=-1, keepdims=True))
        p = jnp.exp(s - m_new)
        acc = jnp.exp(m - m_new) * acc + jnp.dot(p.astype(BF16), v, preferred_element_type=F32)
        return m_new, acc

    init = (jnp.full((tq, 1), -jnp.inf, F32), jnp.zeros((tq, HP), F32))
    _, acc = lax.fori_loop(0, nk, body, init)
    lane = lax.broadcasted_iota(I32, acc.shape, 1)
    o_ref[...] = jnp.where(lane < MLA_V, acc / acc[:, MLA_V:MLA_V + 1], 0.0)


def _mla_attention(q, k, v, batch, seq, tq=512, tk=1024):
    nq = seq // tq
    return pl.pallas_call(
        functools.partial(_mla_kernel, tk=tk), name="mla_attention",
        grid=(batch, MLA_HEADS, nq),
        in_specs=[pl.BlockSpec((tq, HP), lambda b, h, i: (b * nq + i, h)),
                  pl.BlockSpec((seq, HP), lambda b, h, i: (b, h)),
                  pl.BlockSpec((seq, HP), lambda b, h, i: (b, h))],
        out_specs=pl.BlockSpec((tq, HP), lambda b, h, i: (b * nq + i, h)),
        out_shape=jax.ShapeDtypeStruct((batch * seq, MLA_QK), F32),
        compiler_params=_cparams("parallel", "parallel", "arbitrary"),
    )(q, k, v)


def _swa_kernel(sink_ref, q_ref, kp_ref, kc_ref, kn_ref, vp_ref, vc_ref, vn_ref, o_ref, *, seq):
    w = SWA_WINDOW
    tq = q_ref.shape[0]
    i = pl.program_id(1)
    kcat = jnp.concatenate([kp_ref[...], kc_ref[...], kn_ref[...]], axis=0)
    vcat = jnp.concatenate([vp_ref[...], vc_ref[...], vn_ref[...]], axis=0)
    for j in range(tq // w):
        qpos = i * tq + j * w + lax.broadcasted_iota(I32, (w, 3 * w), 0)
        kpos = i * tq + (j - 1) * w + lax.broadcasted_iota(I32, (w, 3 * w), 1)
        valid = (jnp.abs(kpos - qpos) <= w) & (kpos >= 0) & (kpos < seq)
        for h in range(SWA_Q_HEADS):
            kh = h // (SWA_Q_HEADS // SWA_KV_HEADS)
            qh = q_ref[j * w:(j + 1) * w, h * HP:(h + 1) * HP]
            kj = kcat[j * w:(j + 3) * w, kh * HP:(kh + 1) * HP]
            vj = vcat[j * w:(j + 3) * w, kh * HP:(kh + 1) * HP]
            s = lax.dot_general(qh, kj, (((1,), (1,)), ((), ())), preferred_element_type=F32)
            s = jnp.where(valid, s, NEG_BIG)
            sink = sink_ref[0, h]
            m = jnp.maximum(jnp.max(s, axis=-1, keepdims=True), sink)
            p = jnp.exp(s - m)
            den = jnp.sum(p, axis=-1, keepdims=True) + jnp.exp(sink - m)
            o_ref[j * w:(j + 1) * w, h * HP:(h + 1) * HP] = jnp.dot(p.astype(BF16), vj, preferred_element_type=F32) / den


def _swa_attention(sq, sk, sv, sink, batch, seq, tq=512):
    w = SWA_WINDOW
    nq, nw, r = seq // tq, seq // w, tq // w
    prev = pl.BlockSpec((w, SWA_KV), lambda b, i: (b * nw + jnp.maximum(i * r - 1, 0), 0))
    cur = pl.BlockSpec((tq, SWA_KV), lambda b, i: (b * nq + i, 0))
    nxt = pl.BlockSpec((w, SWA_KV), lambda b, i: (b * nw + jnp.minimum((i + 1) * r, nw - 1), 0))
    return pl.pallas_call(
        functools.partial(_swa_kernel, seq=seq), name="swa_attention",
        grid=(batch, nq),
        in_specs=[pl.BlockSpec(memory_space=pltpu.SMEM),
                  pl.BlockSpec((tq, SWA_Q), lambda b, i: (b * nq + i, 0)),
                  prev, cur, nxt, prev, cur, nxt],
        out_specs=pl.BlockSpec((tq, SWA_Q), lambda b, i: (b * nq + i, 0)),
        out_shape=jax.ShapeDtypeStruct((batch * seq, SWA_Q), F32),
        compiler_params=_cparams("parallel", "parallel"),
    )(sink, sq, sk, sk, sk, sv, sv, sv)


def _lru_kernel(xpf_ref, xcf_ref, xnf_ref, xpb_ref, xcb_ref, xnb_ref, cw_ref, cb_ref, wf_ref, bf_ref, wb_ref, bb_ref,
                lam_ref, hf_out, hb_out, carry_f, carry_b, *, nt):
    tm = xcf_ref.shape[0]
    i = pl.program_id(1)
    halo = xpf_ref.shape[0]

    @pl.when(i == 0)
    def _():
        carry_f[...] = jnp.zeros_like(carry_f)
        carry_b[...] = jnp.zeros_like(carry_b)

    def gates(xp_ref, xc_ref, xn_ref, first, last, w_ref, b_ref, lam):
        prev = jnp.where(first, 0.0, xp_ref[...])
        nxt = jnp.where(last, 0.0, xn_ref[...])
        xcat = jnp.concatenate([prev, xc_ref[...], nxt], axis=0)
        cw = cw_ref[...]
        conv = cb_ref[...]
        for tap in range(cw.shape[0]):
            conv = conv + cw[tap:tap + 1, :] * xcat[halo - 1 + tap: halo - 1 + tap + tm, :]
        g = jnp.dot(conv.astype(BF16), w_ref[...], preferred_element_type=F32) + b_ref[...]
        r = _sigmoid(g[:, :LRU_WIDTH])
        gate_i = _sigmoid(g[:, LRU_WIDTH:])
        softplus = jnp.maximum(-lam, 0.0) + jnp.log1p(jnp.exp(-jnp.abs(lam)))
        log_a = -LRU_C * r * softplus
        a = jnp.exp(log_a)
        b = jnp.sqrt(1.0 - a * a) * (gate_i * conv)
        return a, b

    row = lax.broadcasted_iota(I32, (tm, LRU_WIDTH), 0)

    def scan(a, b, reverse):
        k = 1
        while k < tm:
            if reverse:
                keep = row < tm - k
                shift = tm - k
            else:
                keep = row >= k
                shift = k
            a_s = jnp.where(keep, pltpu.roll(a, shift, axis=0), 1.0)
            b_s = jnp.where(keep, pltpu.roll(b, shift, axis=0), 0.0)
            b = a * b_s + b
            a = a * a_s
            k *= 2
        return a, b

    a, b = gates(xpf_ref, xcf_ref, xnf_ref, i == 0, i == nt - 1, wf_ref, bf_ref, lam_ref[0:1, :])
    a, b = scan(a, b, False)
    h = a * carry_f[...] + b
    hf_out[...] = h
    carry_f[...] = h[tm - 1:tm, :]

    a, b = gates(xpb_ref, xcb_ref, xnb_ref, i == nt - 1, i == 0, wb_ref, bb_ref, lam_ref[1:2, :])
    a, b = scan(a, b, True)
    h = a * carry_b[...] + b
    hb_out[...] = h
    carry_b[...] = h[0:1, :]


def _lru_scan(lx, conv_w, conv_b, w_gate, b_gate, lam, batch, seq, tm=256):
    nt, hb = seq // tm, seq // SUBLANES
    r = tm // SUBLANES
    fwd = lambda b, i: i
    bwd = lambda b, i: nt - 1 - i

    def specs(tile):
        return [pl.BlockSpec((SUBLANES, LRU_WIDTH), lambda b, i: (b * hb + jnp.maximum(tile(b, i) * r - 1, 0), 0)),
                pl.BlockSpec((tm, LRU_WIDTH), lambda b, i: (b * nt + tile(b, i), 0)),
                pl.BlockSpec((SUBLANES, LRU_WIDTH), lambda b, i: (b * hb + jnp.minimum((tile(b, i) + 1) * r, hb - 1), 0))]

    full = lambda a: pl.BlockSpec(a.shape, lambda b, i: (0,) * a.ndim)
    return pl.pallas_call(
        functools.partial(_lru_kernel, nt=nt), name="lru_scan",
        grid=(batch, nt),
        in_specs=specs(fwd) + specs(bwd) + [full(conv_w), full(conv_b), full(w_gate[0]), full(b_gate[0]),
                                            full(w_gate[1]), full(b_gate[1]), full(lam)],
        out_specs=[pl.BlockSpec((tm, LRU_WIDTH), lambda b, i: (b * nt + i, 0)),
                   pl.BlockSpec((tm, LRU_WIDTH), lambda b, i: (b * nt + nt - 1 - i, 0))],
        out_shape=[jax.ShapeDtypeStruct((batch * seq, LRU_WIDTH), F32)] * 2,
        scratch_shapes=[pltpu.VMEM((1, LRU_WIDTH), F32), pltpu.VMEM((1, LRU_WIDTH), F32)],
        compiler_params=_cparams("parallel", "arbitrary"),
    )(lx, lx, lx, lx, lx, lx, conv_w, conv_b, w_gate[0], b_gate[0], w_gate[1], b_gate[1], lam)


def _out_proj_kernel(x_ref, oa_ref, hf_ref, hb_ref, lg_ref, oc_ref, ga_ref, gb_ref, gc_ref, wa_ref, wb_ref, wc_ref,
                     nffn_ref, wq_ref, x1_out, xn_out, q_out):
    mix_a = _rmsnorm(oa_ref[...], ga_ref[...], MLA_HEADS * MLA_V).astype(BF16)
    o_lru = (hf_ref[...] + hb_ref[...]) * _gelu(lg_ref[...])
    mix_b = _rmsnorm(o_lru, gb_ref[...], LRU_WIDTH).astype(BF16)
    mix_c = _rmsnorm(oc_ref[...], gc_ref[...], SWA_Q_HEADS * SWA_HEAD_DIM).astype(BF16)
    x1 = (x_ref[...]
          + jnp.dot(mix_a, wa_ref[...], preferred_element_type=F32)
          + jnp.dot(mix_b, wb_ref[...], preferred_element_type=F32)
          + jnp.dot(mix_c, wc_ref[...], preferred_element_type=F32))
    x1_out[...] = x1
    xn = _rmsnorm(x1, nffn_ref[...], D_MODEL)
    xn_out[...] = xn
    q_out[...] = jnp.dot(xn.astype(BF16), wq_ref[...], preferred_element_type=F32)


def _out_proj(x, o_mla, h_f, h_b, l_g, o_swa, ow, nffn, wq, tm=256):
    t = x.shape[0]
    row = lambda w: pl.BlockSpec((tm, w), lambda i: (i, 0))
    full = lambda a: pl.BlockSpec(a.shape, lambda i: (0,) * a.ndim)
    nq = wq.shape[1]
    return pl.pallas_call(
        _out_proj_kernel, name="out_proj",
        grid=(t // tm,),
        in_specs=[row(D_MODEL), row(MLA_QK), row(LRU_WIDTH), row(LRU_WIDTH), row(LRU_WIDTH), row(SWA_Q),
                  full(ow["g_a"]), full(ow["g_b"]), full(ow["g_c"]), full(ow["w_a"]), full(ow["w_b"]), full(ow["w_c"]),
                  full(nffn), full(wq)],
        out_specs=[row(D_MODEL), row(D_MODEL), row(nq)],
        out_shape=[jax.ShapeDtypeStruct((t, D_MODEL), F32), jax.ShapeDtypeStruct((t, D_MODEL), F32),
                   jax.ShapeDtypeStruct((t, nq), F32)],
        compiler_params=_cparams("parallel"),
    )(x, o_mla, h_f, h_b, l_g, o_swa, ow["g_a"], ow["g_b"], ow["g_c"], ow["w_a"], ow["w_b"], ow["w_c"], nffn, wq)


_INT_MAX = 2 ** 31 - 1


def _top_k_rows(s, k, tag=None):
    if tag is None:
        tag = lax.broadcasted_iota(I32, s.shape, 0)
    vals, picks = [], []
    for _ in range(k):
        m = jnp.max(s, axis=0, keepdims=True)
        idx = jnp.min(jnp.where(s == m, tag, _INT_MAX), axis=0, keepdims=True)
        vals.append(m)
        picks.append(idx)
        s = jnp.where(tag == idx, -jnp.inf, s)
    return jnp.concatenate(vals, axis=0), jnp.concatenate(picks, axis=0)


def _candidates(v1, v2):
    k, tm = v1.shape
    row8 = lax.broadcasted_iota(I32, (SUBLANES, tm), 0)
    row16 = lax.broadcasted_iota(I32, (k, tm), 0)
    sums = [v1[0:1] + v2, v1[1:2] + v2[0:SUBLANES], v1[SUBLANES:] + v2[0:1]]
    tags = [row16, k + row8, (row8 + SUBLANES) * k]
    mid = 3 * SUBLANES
    rowm = lax.broadcasted_iota(I32, (mid, tm), 0)
    arow = jnp.full((mid, tm), -1, I32)
    brow = jnp.zeros((mid, tm), I32)
    start = 0
    for a in range(2, SUBLANES):
        nb = k // (a + 1)
        inrun = (rowm >= start) & (rowm < start + nb)
        arow = jnp.where(inrun, a, arow)
        brow = jnp.where(inrun, rowm - start, brow)
        start += nb
    part_a = jnp.zeros((mid, tm), F32)
    part_b = jnp.zeros((mid, tm), F32)
    for a in range(2, SUBLANES):
        part_a = jnp.where(arow == a, v1[a:a + 1], part_a)
    for b in range(k // 3):
        part_b = jnp.where(brow == b, v2[b:b + 1], part_b)
    used = arow >= 0
    sums.append(jnp.where(used, part_a + part_b, -jnp.inf))
    tags.append(jnp.where(used, arow * k + brow, _INT_MAX))
    return jnp.concatenate(sums, axis=0), jnp.concatenate(tags, axis=0)


def _take_rows(table, idx):
    out = jnp.zeros(idx.shape, table.dtype)
    for a in range(table.shape[0]):
        out = jnp.where(idx == a, table[a:a + 1], out)
    return out


def _peer_topk_kernel(q_ref, keys_ref, id_out, gate_out):
    half = PEER_DKEY // 2
    dn = (((1,), (1,)), ((), ()))
    q = q_ref[...]
    s1 = lax.dot_general(keys_ref[0], q[:, :half], dn, preferred_element_type=F32, precision=lax.Precision.HIGHEST)
    s2 = lax.dot_general(keys_ref[1], q[:, half:], dn, preferred_element_type=F32, precision=lax.Precision.HIGHEST)
    v1, i1 = _top_k_rows(s1, PEER_TOPK)
    v2, i2 = _top_k_rows(s2, PEER_TOPK)
    cand, flat = _candidates(v1, v2)
    sc, pick = _top_k_rows(cand, PEER_TOPK, tag=flat)
    e = jnp.exp(sc - sc[0:1, :])
    rank1, rank2 = pick >> 4, pick & (PEER_TOPK - 1)
    id_out[0] = _take_rows(i1, rank1) * PEER_NKEYS + _take_rows(i2, rank2)
    gate_out[0] = e / jnp.sum(e, axis=0, keepdims=True)


def _peer_topk(q, subkeys, tm=256):
    t = q.shape[0]
    out = pl.BlockSpec((1, PEER_TOPK, tm), lambda i, h: (h, 0, i))
    return pl.pallas_call(
        _peer_topk_kernel, name="peer_topk",
        grid=(t // tm, PEER_HEADS),
        in_specs=[pl.BlockSpec((tm, PEER_DKEY), lambda i, h: (i, h)),
                  pl.BlockSpec(subkeys.shape, lambda i, h: (0, 0, 0))],
        out_specs=[out, out],
        out_shape=[jax.ShapeDtypeStruct((PEER_HEADS, PEER_TOPK, t), I32),
                   jax.ShapeDtypeStruct((PEER_HEADS, PEER_TOPK, t), F32)],
        compiler_params=_cparams("parallel", "parallel"),
    )(q, subkeys)


def _fold_sublanes(vals, masks):
    for level, mask in enumerate(masks):
        nxt = []
        for a, b in zip(vals[0::2], vals[1::2]):
            nxt.append(jnp.where(mask, a, b) + pltpu.roll(jnp.where(mask, b, a), 1 << level, axis=0))
        vals = nxt
    (out,) = vals
    return out


def _pack_table(w):
    bits = lax.bitcast_convert_type(w.astype(BF16), jnp.uint16).astype(jnp.uint32)
    half = D_MODEL // 2
    words = lax.bitcast_convert_type((bits[:, half:] << 16) | bits[:, :half], I32)
    return jnp.pad(words.reshape(-1, LANES), ((0, SUBLANES), (0, 0)))


def _unpack(w):
    return pltpu.bitcast(w << 16, F32), pltpu.bitcast(w & -0x10000, F32)


def _peer_hidden_kernel(id_ref, x_ref, u_ref, h_out, r_scr):
    tm = x_ref.shape[0]
    sub = lax.broadcasted_iota(I32, (SUBLANES, LANES), 0)
    sub_masks = [(sub % (2 << lv)) < (1 << lv) for lv in range(3)]
    groups = PEER_SLOTS // SUBLANES
    ones = jnp.ones((SUBLANES, LANES), BF16)
    dn = (((1,), (1,)), ((), ()))

    def token(tt, g):
        t = g * SUBLANES + tt
        xt = x_ref[t]
        x_lo = jnp.where(sub < PACK_ROWS, xt, 0.0)
        x_hi = jnp.where(sub < PACK_ROWS, pltpu.roll(xt, PACK_ROWS, axis=0), 0.0)
        for j in range(groups):
            prods = []
            for s in range(SUBLANES):
                row = pl.multiple_of(id_ref[t * PEER_SLOTS + j * SUBLANES + s], PACK_ROWS)
                lo, hi = _unpack(u_ref[pl.ds(row, SUBLANES), :])
                prods.append(lo * x_lo + hi * x_hi)
            r_scr[pl.ds(pl.multiple_of(tt * PEER_SLOTS + j * SUBLANES, SUBLANES), SUBLANES), :] = _fold_sublanes(prods, sub_masks)
        return g

    def group(g, _):
        lax.fori_loop(0, SUBLANES, token, g)
        r = r_scr[...]
        hi = r.astype(BF16)
        lo = (r - hi.astype(F32)).astype(BF16)
        sums = (lax.dot_general(ones, hi, dn, preferred_element_type=F32)
                + lax.dot_general(ones, lo, dn, preferred_element_type=F32))
        out = jnp.zeros((SUBLANES, LANES), F32)
        for tt in range(SUBLANES):
            out = jnp.where(sub == tt, sums[:, tt * PEER_SLOTS:(tt + 1) * PEER_SLOTS], out)
        h_out[pl.ds(pl.multiple_of(g * SUBLANES, SUBLANES), SUBLANES), :] = out
        return 0

    lax.fori_loop(0, tm // SUBLANES, group, 0)


def _smem_rows(tm):
    return pl.BlockSpec((tm * PEER_SLOTS,), lambda i: (i,), memory_space=pltpu.SMEM)


def _table_spec():
    return pl.BlockSpec((TABLE_ROWS, LANES), lambda i: (0, 0), pipeline_mode=pl.Buffered(1))


def _peer_hidden(rows, x3, u_packed, tm=128):
    t = x3.shape[0]
    return pl.pallas_call(
        _peer_hidden_kernel, name="peer_hidden",
        grid=(t // tm,),
        in_specs=[_smem_rows(tm), pl.BlockSpec((tm, SUBLANES, LANES), lambda i: (i, 0, 0)), _table_spec()],
        out_specs=pl.BlockSpec((tm, PEER_SLOTS), lambda i: (i, 0)),
        out_shape=jax.ShapeDtypeStruct((t, PEER_SLOTS), F32),
        scratch_shapes=[pltpu.VMEM((SUBLANES * PEER_SLOTS, LANES), F32)],
        compiler_params=_cparams("arbitrary"),
    )(rows, x3, u_packed)


def _peer_coef_kernel(h_ref, gate_ref, c_out):
    ct = (gate_ref[...] * _gelu(h_ref[...])).T
    for g in range(ct.shape[1] // SUBLANES):
        c_out[g] = ct if g == 0 else pltpu.roll(ct, LANES - SUBLANES * g, axis=1)


def _peer_coef(hid, gate, tm=LANES):
    t = gate.shape[0]
    return pl.pallas_call(
        _peer_coef_kernel, name="peer_coef",
        grid=(t // tm,),
        in_specs=[pl.BlockSpec((tm, PEER_SLOTS), lambda i: (i, 0)), pl.BlockSpec((tm, PEER_SLOTS), lambda i: (i, 0))],
        out_specs=pl.BlockSpec((tm // SUBLANES, PEER_SLOTS, LANES), lambda i: (i, 0, 0)),
        out_shape=jax.ShapeDtypeStruct((t // SUBLANES, PEER_SLOTS, LANES), F32),
        compiler_params=_cparams("parallel"),
    )(hid, gate)


def _peer_value_kernel(id_ref, c_ref, v_ref, y_out, cb_scr):
    tm = y_out.shape[0]
    sub = lax.broadcasted_iota(I32, (SUBLANES, LANES), 0)

    def group(g, _):
        cg = c_ref[g]
        for tt in range(SUBLANES):
            cb_scr[tt * PEER_SLOTS:(tt + 1) * PEER_SLOTS, :] = jnp.broadcast_to(cg[:, tt:tt + 1], (PEER_SLOTS, LANES))

        def token(tt, _):
            t = g * SUBLANES + tt

            acc_lo = jnp.zeros((SUBLANES, LANES), F32)
            acc_hi = jnp.zeros((SUBLANES, LANES), F32)
            for j in range(PEER_SLOTS // SUBLANES):
                t_lo, t_hi = [], []
                for s in range(SUBLANES):
                    e = j * SUBLANES + s
                    row = pl.multiple_of(id_ref[t * PEER_SLOTS + e], PACK_ROWS)
                    c = jnp.broadcast_to(cb_scr[pl.ds(tt * PEER_SLOTS + e, 1), :], (SUBLANES, LANES))
                    lo, hi = _unpack(v_ref[pl.ds(row, SUBLANES), :])
                    t_lo.append(c * lo)
                    t_hi.append(c * hi)
                while len(t_lo) > 1:
                    t_lo = [a + b for a, b in zip(t_lo[0::2], t_lo[1::2])]
                    t_hi = [a + b for a, b in zip(t_hi[0::2], t_hi[1::2])]
                acc_lo = acc_lo + t_lo[0]
                acc_hi = acc_hi + t_hi[0]
            y_out[t] = jnp.where(sub < PACK_ROWS, acc_lo, pltpu.roll(acc_hi, PACK_ROWS, axis=0))
            return 0

        lax.fori_loop(0, SUBLANES, token, 0)
        return 0

    lax.fori_loop(0, tm // SUBLANES, group, 0)


def _peer_value(rows, coef, v_packed, tm=128):
    t = coef.shape[0] * SUBLANES
    return pl.pallas_call(
        _peer_value_kernel, name="peer_value",
        grid=(t // tm,),
        in_specs=[_smem_rows(tm), pl.BlockSpec((tm // SUBLANES, PEER_SLOTS, LANES), lambda i: (i, 0, 0)), _table_spec()],
        out_specs=pl.BlockSpec((tm, SUBLANES, LANES), lambda i: (i, 0, 0)),
        out_shape=jax.ShapeDtypeStruct((t, SUBLANES, LANES), F32),
        scratch_shapes=[pltpu.VMEM((SUBLANES * PEER_SLOTS, LANES), F32)],
        compiler_params=_cparams("arbitrary"),
    )(rows, coef, v_packed)


def _residual_kernel(x_ref, y_ref, g_ref, o_ref, *, final):
    x = x_ref[...] + y_ref[...]
    o_ref[...] = _rmsnorm(x, g_ref[...], D_MODEL) if final else x


def _residual(x1, y, gain, final, tm=512):
    t = x1.shape[0]
    return pl.pallas_call(
        functools.partial(_residual_kernel, final=final), name="peer_residual",
        grid=(t // tm,),
        in_specs=[pl.BlockSpec((tm, D_MODEL), lambda i: (i, 0)),
                  pl.BlockSpec((tm, D_MODEL), lambda i: (i, 0)),
                  pl.BlockSpec((1, D_MODEL), lambda i: (0, 0))],
        out_specs=pl.BlockSpec((tm, D_MODEL), lambda i: (i, 0)),
        out_shape=jax.ShapeDtypeStruct((t, D_MODEL), F32),
        compiler_params=_cparams("parallel"),
    )(x1, y, gain)


def _peer_ffn(x1, xn, q, subkeys, u, v, gain, final):
    t = x1.shape[0]
    ids_h, gate_h = _peer_topk(q, subkeys)
    ids = ids_h.reshape(PEER_SLOTS, t).T
    gate = gate_h.reshape(PEER_SLOTS, t).T
    rows = (ids * PACK_ROWS).reshape(-1)
    hid = _peer_hidden(rows, xn.reshape(t, SUBLANES, LANES), _pack_table(u))
    coef = _peer_coef(hid, gate)
    y = _peer_value(rows, coef, _pack_table(v))
    return _residual(x1, y.reshape(t, D_MODEL), gain, final)


def kernel(x, positions, norm_mix, w_in, q_norm, w_uq, kv_norm, w_ukv, conv_w, conv_b, lru_wa, lru_ba, lru_wx, lru_bx,
           lru_lambda, swa_sink, grp_norm, w_out, norm_ffn, peer_wq, peer_subkeys, peer_u, peer_v, norm_final):
    batch, seq, _ = x.shape
    t = batch * seq
    depth = w_in.shape[0]
    xt = x.reshape(t, D_MODEL)

    pos = positions.astype(F32).reshape(t, 1)
    zeros = lambda n: jnp.zeros((n,), F32)
    inv_m, inv_s = _inv_freq(MLA_ROPE), _inv_freq(SWA_HEAD_DIM)
    inv_q = jnp.tile(jnp.concatenate([zeros(MLA_NOPE), inv_m, inv_m, zeros(HP - MLA_NOPE - MLA_ROPE)]), MLA_HEADS)
    inv_w = jnp.tile(jnp.concatenate([inv_s, inv_s, zeros(HP - SWA_HEAD_DIM)]), SWA_Q_HEADS)
    cosq, sinq = _rope_tables(pos, inv_q[None, :])
    coss, sins = _rope_tables(pos, inv_w[None, :])

    for l in range(depth):
        w_in_p, wq, wkv, w_gate, b_gate, ow = _layer_weights(
            l, w_in, w_uq, w_ukv, lru_wa, lru_ba, lru_wx, lru_bx, grp_norm, w_out)
        q, k, v, sq, sk, sv, lx, lg = _in_proj(
            xt, norm_mix[l][None, :], w_in_p, q_norm[l][None, :], wq, kv_norm[l][None, :], wkv, cosq, sinq, coss, sins)
        o_mla = _mla_attention(q, k, v, batch, seq)
        o_swa = _swa_attention(sq, sk, sv, swa_sink[l][None, :], batch, seq)
        h_f, h_b = _lru_scan(lx, conv_w[l], conv_b[l][None, :], w_gate, b_gate, lru_lambda[l], batch, seq)
        x1, xn, pq = _out_proj(xt, o_mla, h_f, h_b, lg, o_swa, ow, norm_ffn[l][None, :], peer_wq[l].astype(BF16))
        final = l == depth - 1
        gain = norm_final[None, :] if final else jnp.ones((1, D_MODEL), F32)
        xt = _peer_ffn(x1, xn, pq, peer_subkeys[l], peer_u[l], peer_v[l], gain, final)
    return xt.reshape(batch, seq, D_MODEL)
```

```python
import functools

import jax
import jax.numpy as jnp
from jax import lax
from jax.experimental import pallas as pl
from jax.experimental.pallas import tpu as pltpu

F32 = jnp.float32
BF16 = jnp.bfloat16
I32 = jnp.int32

D_MODEL = 1024
MLA_HEADS, MLA_NOPE, MLA_ROPE, MLA_V = 6, 64, 32, 64
MLA_Q_RANK, MLA_KV_RANK = 256, 128
LRU_WIDTH, LRU_BLOCKS, LRU_C = 384, 6, 8.0
SWA_Q_HEADS, SWA_KV_HEADS, SWA_HEAD_DIM, SWA_WINDOW = 4, 2, 64, 128
ROPE_THETA, EPS, NEG_BIG = 10000.0, 1e-6, -1e30
PEER_HEADS, PEER_NKEYS, PEER_DKEY, PEER_TOPK = 8, 128, 256, 16
PEER_SLOTS = PEER_HEADS * PEER_TOPK
PEER_EXPERTS = PEER_NKEYS * PEER_NKEYS

LANES = 128
SUBLANES = 8
HP = LANES
MLA_QK = MLA_HEADS * HP
SWA_Q = SWA_Q_HEADS * HP
SWA_KV = SWA_KV_HEADS * HP
VMEM_LIMIT = 56 * 1024 * 1024
PACK_ROWS = D_MODEL // (2 * LANES)
TABLE_ROWS = PEER_EXPERTS * PACK_ROWS + SUBLANES

_C_Q, _C_KV, _S_Q, _S_QR, _S_K, _S_KR, _S_V, _L_X, _L_G, _K_R, _IN_COLS = (
    0, 256, 384, 896, 1408, 1664, 1920, 2176, 2560, 2944, 3072)


def _cparams(*sem):
    return pltpu.CompilerParams(dimension_semantics=sem, vmem_limit_bytes=VMEM_LIMIT)


def _rmsnorm(x, g, n):
    return x * lax.rsqrt(jnp.sum(x * x, axis=-1, keepdims=True) * (1.0 / n) + EPS) * g


def _sigmoid(x):
    return 1.0 / (1.0 + jnp.exp(-x))


def _gelu(x):
    return 0.5 * x * (1.0 + lax.erf(x * (2.0 ** -0.5)))


def _rot_half_cols(w, d):
    k = w.shape[0]
    w3 = w.reshape(k, -1, d)
    return jnp.concatenate([-w3[..., d // 2:], w3[..., : d // 2]], axis=-1).reshape(k, -1)


def _pad_heads(w, d, lead=0):
    k = w.shape[0]
    w3 = w.reshape(k, -1, d)
    return jnp.pad(w3, ((0, 0), (0, 0), (lead, HP - d - lead))).reshape(k, -1)


def _inv_freq(d):
    return ROPE_THETA ** (-jnp.arange(0, d, 2, dtype=F32) / d)


def _layer_weights(l, w_in, w_uq, w_ukv, lru_wa, lru_ba, lru_wx, lru_bx, grp_norm, w_out):
    wi = w_in[l]
    c_q, c_kv, k_r = wi[:, 0:256], wi[:, 256:384], wi[:, 384:416]
    s_q, s_k, s_v = wi[:, 416:672], wi[:, 672:800], wi[:, 800:928]
    l_x, l_g = wi[:, 928:1312], wi[:, 1312:1696]
    k_r_blk = jnp.concatenate([jnp.zeros((D_MODEL, MLA_NOPE), F32), k_r, _rot_half_cols(k_r, MLA_ROPE)], axis=1)
    w_in_p = jnp.concatenate([
        c_q, c_kv,
        _pad_heads(s_q, SWA_HEAD_DIM), _pad_heads(_rot_half_cols(s_q, SWA_HEAD_DIM), SWA_HEAD_DIM),
        _pad_heads(s_k, SWA_HEAD_DIM), _pad_heads(_rot_half_cols(s_k, SWA_HEAD_DIM), SWA_HEAD_DIM),
        _pad_heads(s_v, SWA_HEAD_DIM), l_x, l_g, k_r_blk], axis=1).astype(BF16)

    wq3 = w_uq[l].reshape(MLA_Q_RANK, MLA_HEADS, MLA_NOPE + MLA_ROPE)
    q_nope = wq3[..., :MLA_NOPE].reshape(MLA_Q_RANK, -1)
    q_rope = wq3[..., MLA_NOPE:].reshape(MLA_Q_RANK, -1)
    wq = jnp.concatenate([
        _pad_heads(q_nope, MLA_NOPE) + _pad_heads(q_rope, MLA_ROPE, lead=MLA_NOPE),
        _pad_heads(_rot_half_cols(q_rope, MLA_ROPE), MLA_ROPE, lead=MLA_NOPE)], axis=1).astype(BF16)

    wkv3 = w_ukv[l].reshape(MLA_KV_RANK, MLA_HEADS, MLA_NOPE + MLA_V)
    wkv = jnp.concatenate([
        _pad_heads(wkv3[..., :MLA_NOPE].reshape(MLA_KV_RANK, -1), MLA_NOPE),
        _pad_heads(wkv3[..., MLA_NOPE:].reshape(MLA_KV_RANK, -1), MLA_V)], axis=1).astype(BF16)

    def blockdiag(w):
        n, bi, bj = w.shape
        eye = jnp.eye(n, dtype=w.dtype)
        return (w[:, :, None, :] * eye[:, None, :, None]).reshape(n * bi, n * bj)

    w_gate = [jnp.concatenate([blockdiag(lru_wa[l, d]), blockdiag(lru_wx[l, d])], axis=1).astype(BF16) for d in range(2)]
    b_gate = [jnp.concatenate([lru_ba[l, d].reshape(1, -1), lru_bx[l, d].reshape(1, -1)], axis=1) for d in range(2)]

    g = grp_norm[l]
    wo = w_out[l]
    ga, gb = MLA_HEADS * MLA_V, MLA_HEADS * MLA_V + LRU_WIDTH

    def pad_rows(w, d):
        return jnp.pad(w.reshape(-1, d, w.shape[-1]), ((0, 0), (0, HP - d), (0, 0))).reshape(-1, w.shape[-1])

    out_w = dict(
        g_a=_pad_heads(g[None, :ga], MLA_V), g_b=g[None, ga:gb], g_c=_pad_heads(g[None, gb:], SWA_HEAD_DIM),
        w_a=pad_rows(wo[:ga], MLA_V).astype(BF16), w_b=wo[ga:gb].astype(BF16),
        w_c=pad_rows(wo[gb:], SWA_HEAD_DIM).astype(BF16))
    return w_in_p, wq, wkv, w_gate, b_gate, out_w


def _rope_table_kernel(pos_ref, inv_ref, cos_ref, sin_ref):
    ang = pos_ref[...] * inv_ref[...]
    cos_ref[...] = jnp.cos(ang)
    sin_ref[...] = jnp.sin(ang)


def _rope_tables(pos, inv, tm=512):
    t, w = pos.shape[0], inv.shape[1]
    return pl.pallas_call(
        _rope_table_kernel, name="rope_tables",
        grid=(t // tm,),
        in_specs=[pl.BlockSpec((tm, 1), lambda i: (i, 0)), pl.BlockSpec((1, w), lambda i: (0, 0))],
        out_specs=[pl.BlockSpec((tm, w), lambda i: (i, 0))] * 2,
        out_shape=[jax.ShapeDtypeStruct((t, w), F32)] * 2,
        compiler_params=_cparams("parallel"),
    )(pos, inv)


def _in_proj_kernel(x_ref, nmix_ref, win_ref, qn_ref, wq_ref, kvn_ref, wkv_ref, cos_ref, sin_ref,
                    q_out, k_out, v_out, sq_out, sk_out, sv_out, lx_out, lg_out):
    xn = _rmsnorm(x_ref[...], nmix_ref[...], D_MODEL)
    proj = jnp.dot(xn.astype(BF16), win_ref[...], preferred_element_type=F32)
    cos, sin = cos_ref[...], sin_ref[...]
    cosq = jnp.concatenate([cos[:, :HP]] * MLA_HEADS, axis=1)
    sinq = jnp.concatenate([sin[:, :HP]] * MLA_HEADS, axis=1)
    coss = jnp.concatenate([cos[:, HP:]] * SWA_Q_HEADS, axis=1)
    sins = jnp.concatenate([sin[:, HP:]] * SWA_Q_HEADS, axis=1)

    swa_scale = SWA_HEAD_DIM ** -0.5
    sq_out[...] = ((proj[:, _S_Q:_S_QR] * coss + proj[:, _S_QR:_S_K] * sins) * swa_scale).astype(BF16)
    sk_out[...] = (proj[:, _S_K:_S_KR] * coss[:, :SWA_KV] + proj[:, _S_KR:_S_V] * sins[:, :SWA_KV]).astype(BF16)
    sv_out[...] = proj[:, _S_V:_L_X].astype(BF16)
    lx_out[...] = proj[:, _L_X:_L_G]
    lg_out[...] = proj[:, _L_G:_K_R]

    cqn = _rmsnorm(proj[:, _C_Q:_C_KV], qn_ref[...], MLA_Q_RANK).astype(BF16)
    qq = jnp.dot(cqn, wq_ref[...], preferred_element_type=F32)
    mla_scale = (MLA_NOPE + MLA_ROPE) ** -0.5
    q_out[...] = ((qq[:, :MLA_QK] * cosq + qq[:, MLA_QK:] * sinq) * mla_scale).astype(BF16)

    kvn = _rmsnorm(proj[:, _C_KV:_S_Q], kvn_ref[...], MLA_KV_RANK).astype(BF16)
    kv = jnp.dot(kvn, wkv_ref[...], preferred_element_type=F32)
    kr = proj[:, _K_R:_IN_COLS]
    lane = lax.broadcasted_iota(I32, kr.shape, 1)
    in_rope = (lane >= MLA_NOPE) & (lane < MLA_NOPE + MLA_ROPE)
    kr_rot = jnp.where(in_rope, kr * cosq[:, :HP] + pltpu.roll(kr, HP - MLA_ROPE, axis=1) * sinq[:, :HP], 0.0)
    for h in range(MLA_HEADS):
        k_out[:, h * HP:(h + 1) * HP] = (kv[:, h * HP:(h + 1) * HP] + kr_rot).astype(BF16)
    vlane = lax.broadcasted_iota(I32, (kv.shape[0], MLA_QK), 1)
    v_out[...] = jnp.where(vlane % HP == MLA_V, 1.0, kv[:, MLA_QK:]).astype(BF16)


def _in_proj(x, nmix, w_in_p, qn, wq, kvn, wkv, cos, sin, tm=256):
    t = x.shape[0]
    row = lambda w: pl.BlockSpec((tm, w), lambda i: (i, 0))
    full = lambda a: pl.BlockSpec(a.shape, lambda i: (0,) * a.ndim)
    widths = (MLA_QK, MLA_QK, MLA_QK, SWA_Q, SWA_KV, SWA_KV, LRU_WIDTH, LRU_WIDTH)
    dtypes = (BF16, BF16, BF16, BF16, BF16, BF16, F32, F32)
    return pl.pallas_call(
        _in_proj_kernel, name="in_proj",
        grid=(t // tm,),
        in_specs=[row(D_MODEL), full(nmix), full(w_in_p), full(qn), full(wq), full(kvn), full(wkv),
                  row(2 * HP), row(2 * HP)],
        out_specs=[row(w) for w in widths],
        out_shape=[jax.ShapeDtypeStruct((t, w), d) for w, d in zip(widths, dtypes)],
        compiler_params=_cparams("parallel"),
    )(x, nmix, w_in_p, qn, wq, kvn, wkv, cos, sin)


def _mla_kernel(q_ref, k_ref, v_ref, o_ref, *, tk):
    q = q_ref[...]
    tq = q.shape[0]
    nk = k_ref.shape[0] // tk

    def body(j, carry):
        m, acc = carry
        off = pl.multiple_of(j * tk, tk)
        k = k_ref[pl.ds(off, tk), :]
        v = v_ref[pl.ds(off, tk), :]
        s = lax.dot_general(q, k, (((1,), (1,)), ((), ())), preferred_element_type=F32)
        m_new = jnp.maximum(m, jnp.max(s, axis=-1, keepdims=True))
        p = jnp.exp(s - m_new)
        acc = jnp.exp(m - m_new) * acc + jnp.dot(p.astype(BF16), v, preferred_element_type=F32)
        return m_new, acc

    init = (jnp.full((tq, 1), -jnp.inf, F32), jnp.zeros((tq, HP), F32))
    _, acc = lax.fori_loop(0, nk, body, init)
    lane = lax.broadcasted_iota(I32, acc.shape, 1)
    o_ref[...] = jnp.where(lane < MLA_V, acc / acc[:, MLA_V:MLA_V + 1], 0.0)


def _mla_attention(q, k, v, batch, seq, tq=2048, tk=1024):
    tq, tk = min(tq, seq), min(tk, seq)
    nq = seq // tq
    return pl.pallas_call(
        functools.partial(_mla_kernel, tk=tk), name="mla_attention",
        grid=(batch, MLA_HEADS, nq),
        in_specs=[pl.BlockSpec((tq, HP), lambda b, h, i: (b * nq + i, h)),
                  pl.BlockSpec((seq, HP), lambda b, h, i: (b, h)),
                  pl.BlockSpec((seq, HP), lambda b, h, i: (b, h))],
        out_specs=pl.BlockSpec((tq, HP), lambda b, h, i: (b * nq + i, h)),
        out_shape=jax.ShapeDtypeStruct((batch * seq, MLA_QK), F32),
        compiler_params=_cparams("parallel", "parallel", "arbitrary"),
    )(q, k, v)


def _swa_kernel(sink_ref, q_ref, kp_ref, kc_ref, kn_ref, vp_ref, vc_ref, vn_ref, o_ref, *, seq):
    w = SWA_WINDOW
    tq = q_ref.shape[0]
    i = pl.program_id(1)
    kcat = jnp.concatenate([kp_ref[...], kc_ref[...], kn_ref[...]], axis=0)
    vcat = jnp.concatenate([vp_ref[...], vc_ref[...], vn_ref[...]], axis=0)
    for j in range(tq // w):
        qpos = i * tq + j * w + lax.broadcasted_iota(I32, (w, 3 * w), 0)
        kpos = i * tq + (j - 1) * w + lax.broadcasted_iota(I32, (w, 3 * w), 1)
        valid = (jnp.abs(kpos - qpos) <= w) & (kpos >= 0) & (kpos < seq)
        for h in range(SWA_Q_HEADS):
            kh = h // (SWA_Q_HEADS // SWA_KV_HEADS)
            qh = q_ref[j * w:(j + 1) * w, h * HP:(h + 1) * HP]
            kj = kcat[j * w:(j + 3) * w, kh * HP:(kh + 1) * HP]
            vj = vcat[j * w:(j + 3) * w, kh * HP:(kh + 1) * HP]
            s = lax.dot_general(qh, kj, (((1,), (1,)), ((), ())), preferred_element_type=F32)
            s = jnp.where(valid, s, NEG_BIG)
            sink = sink_ref[0, h]
            m = jnp.maximum(jnp.max(s, axis=-1, keepdims=True), sink)
            p = jnp.exp(s - m)
            den = jnp.sum(p, axis=-1, keepdims=True) + jnp.exp(sink - m)
            o_ref[j * w:(j + 1) * w, h * HP:(h + 1) * HP] = jnp.dot(p.astype(BF16), vj, preferred_element_type=F32) / den


def _swa_attention(sq, sk, sv, sink, batch, seq, tq=512):
    w = SWA_WINDOW
    nq, nw, r = seq // tq, seq // w, tq // w
    prev = pl.BlockSpec((w, SWA_KV), lambda b, i: (b * nw + jnp.maximum(i * r - 1, 0), 0))
    cur = pl.BlockSpec((tq, SWA_KV), lambda b, i: (b * nq + i, 0))
    nxt = pl.BlockSpec((w, SWA_KV), lambda b, i: (b * nw + jnp.minimum((i + 1) * r, nw - 1), 0))
    return pl.pallas_call(
        functools.partial(_swa_kernel, seq=seq), name="swa_attention",
        grid=(batch, nq),
        in_specs=[pl.BlockSpec(memory_space=pltpu.SMEM),
                  pl.BlockSpec((tq, SWA_Q), lambda b, i: (b * nq + i, 0)),
                  prev, cur, nxt, prev, cur, nxt],
        out_specs=pl.BlockSpec((tq, SWA_Q), lambda b, i: (b * nq + i, 0)),
        out_shape=jax.ShapeDtypeStruct((batch * seq, SWA_Q), F32),
        compiler_params=_cparams("parallel", "parallel"),
    )(sink, sq, sk, sk, sk, sv, sv, sv)


def _lru_kernel(xpf_ref, xcf_ref, xnf_ref, xpb_ref, xcb_ref, xnb_ref, cw_ref, cb_ref, wf_ref, bf_ref, wb_ref, bb_ref,
                lam_ref, hf_out, hb_out, carry_f, carry_b, *, nt):
    tm = xcf_ref.shape[0]
    i = pl.program_id(1)
    halo = xpf_ref.shape[0]

    @pl.when(i == 0)
    def _():
        carry_f[...] = jnp.zeros_like(carry_f)
        carry_b[...] = jnp.zeros_like(carry_b)

    def gates(xp_ref, xc_ref, xn_ref, first, last, w_ref, b_ref, lam):
        prev = jnp.where(first, 0.0, xp_ref[...])
        nxt = jnp.where(last, 0.0, xn_ref[...])
        xcat = jnp.concatenate([prev, xc_ref[...], nxt], axis=0)
        cw = cw_ref[...]
        conv = cb_ref[...]
        for tap in range(cw.shape[0]):
            conv = conv + cw[tap:tap + 1, :] * xcat[halo - 1 + tap: halo - 1 + tap + tm, :]
        g = jnp.dot(conv.astype(BF16), w_ref[...], preferred_element_type=F32) + b_ref[...]
        r = _sigmoid(g[:, :LRU_WIDTH])
        gate_i = _sigmoid(g[:, LRU_WIDTH:])
        softplus = jnp.maximum(-lam, 0.0) + jnp.log1p(jnp.exp(-jnp.abs(lam)))
        log_a = -LRU_C * r * softplus
        a = jnp.exp(log_a)
        b = jnp.sqrt(1.0 - a * a) * (gate_i * conv)
        return a, b

    row = lax.broadcasted_iota(I32, (tm, LRU_WIDTH), 0)

    def scan(a, b, reverse):
        k = 1
        while k < tm:
            if reverse:
                keep = row < tm - k
                shift = tm - k
            else:
                keep = row >= k
                shift = k
            a_s = jnp.where(keep, pltpu.roll(a, shift, axis=0), 1.0)
            b_s = jnp.where(keep, pltpu.roll(b, shift, axis=0), 0.0)
            b = a * b_s + b
            a = a * a_s
            k *= 2
        return a, b

    a, b = gates(xpf_ref, xcf_ref, xnf_ref, i == 0, i == nt - 1, wf_ref, bf_ref, lam_ref[0:1, :])
    a, b = scan(a, b, False)
    h = a * carry_f[...] + b
    hf_out[...] = h
    carry_f[...] = h[tm - 1:tm, :]

    a, b = gates(xpb_ref, xcb_ref, xnb_ref, i == nt - 1, i == 0, wb_ref, bb_ref, lam_ref[1:2, :])
    a, b = scan(a, b, True)
    h = a * carry_b[...] + b
    hb_out[...] = h
    carry_b[...] = h[0:1, :]


def _lru_scan(lx, conv_w, conv_b, w_gate, b_gate, lam, batch, seq, tm=256):
    nt, hb = seq // tm, seq // SUBLANES
    r = tm // SUBLANES
    fwd = lambda b, i: i
    bwd = lambda b, i: nt - 1 - i

    def specs(tile):
        return [pl.BlockSpec((SUBLANES, LRU_WIDTH), lambda b, i: (b * hb + jnp.maximum(tile(b, i) * r - 1, 0), 0)),
                pl.BlockSpec((tm, LRU_WIDTH), lambda b, i: (b * nt + tile(b, i), 0)),
                pl.BlockSpec((SUBLANES, LRU_WIDTH), lambda b, i: (b * hb + jnp.minimum((tile(b, i) + 1) * r, hb - 1), 0))]

    full = lambda a: pl.BlockSpec(a.shape, lambda b, i: (0,) * a.ndim)
    return pl.pallas_call(
        functools.partial(_lru_kernel, nt=nt), name="lru_scan",
        grid=(batch, nt),
        in_specs=specs(fwd) + specs(bwd) + [full(conv_w), full(conv_b), full(w_gate[0]), full(b_gate[0]),
                                            full(w_gate[1]), full(b_gate[1]), full(lam)],
        out_specs=[pl.BlockSpec((tm, LRU_WIDTH), lambda b, i: (b * nt + i, 0)),
                   pl.BlockSpec((tm, LRU_WIDTH), lambda b, i: (b * nt + nt - 1 - i, 0))],
        out_shape=[jax.ShapeDtypeStruct((batch * seq, LRU_WIDTH), F32)] * 2,
        scratch_shapes=[pltpu.VMEM((1, LRU_WIDTH), F32), pltpu.VMEM((1, LRU_WIDTH), F32)],
        compiler_params=_cparams("parallel", "arbitrary"),
    )(lx, lx, lx, lx, lx, lx, conv_w, conv_b, w_gate[0], b_gate[0], w_gate[1], b_gate[1], lam)


def _out_proj_kernel(x_ref, oa_ref, hf_ref, hb_ref, lg_ref, oc_ref, ga_ref, gb_ref, gc_ref, wa_ref, wb_ref, wc_ref,
                     nffn_ref, wq_ref, x1_out, xn_out, q_out):
    mix_a = _rmsnorm(oa_ref[...], ga_ref[...], MLA_HEADS * MLA_V).astype(BF16)
    o_lru = (hf_ref[...] + hb_ref[...]) * _gelu(lg_ref[...])
    mix_b = _rmsnorm(o_lru, gb_ref[...], LRU_WIDTH).astype(BF16)
    mix_c = _rmsnorm(oc_ref[...], gc_ref[...], SWA_Q_HEADS * SWA_HEAD_DIM).astype(BF16)
    x1 = (x_ref[...]
          + jnp.dot(mix_a, wa_ref[...], preferred_element_type=F32)
          + jnp.dot(mix_b, wb_ref[...], preferred_element_type=F32)
          + jnp.dot(mix_c, wc_ref[...], preferred_element_type=F32))
    x1_out[...] = x1
    xn = _rmsnorm(x1, nffn_ref[...], D_MODEL)
    xn_out[...] = xn
    q_out[...] = jnp.dot(xn.astype(BF16), wq_ref[...], preferred_element_type=F32)


def _out_proj(x, o_mla, h_f, h_b, l_g, o_swa, ow, nffn, wq, tm=256):
    t = x.shape[0]
    row = lambda w: pl.BlockSpec((tm, w), lambda i: (i, 0))
    full = lambda a: pl.BlockSpec(a.shape, lambda i: (0,) * a.ndim)
    nq = wq.shape[1]
    return pl.pallas_call(
        _out_proj_kernel, name="out_proj",
        grid=(t // tm,),
        in_specs=[row(D_MODEL), row(MLA_QK), row(LRU_WIDTH), row(LRU_WIDTH), row(LRU_WIDTH), row(SWA_Q),
                  full(ow["g_a"]), full(ow["g_b"]), full(ow["g_c"]), full(ow["w_a"]), full(ow["w_b"]), full(ow["w_c"]),
                  full(nffn), full(wq)],
        out_specs=[row(D_MODEL), row(D_MODEL), row(nq)],
        out_shape=[jax.ShapeDtypeStruct((t, D_MODEL), F32), jax.ShapeDtypeStruct((t, D_MODEL), F32),
                   jax.ShapeDtypeStruct((t, nq), F32)],
        compiler_params=_cparams("parallel"),
    )(x, o_mla, h_f, h_b, l_g, o_swa, ow["g_a"], ow["g_b"], ow["g_c"], ow["w_a"], ow["w_b"], ow["w_c"], nffn, wq)


_INT_MAX = 2 ** 31 - 1


def _top_k_rows(s, k, tag=None):
    if tag is None:
        tag = lax.broadcasted_iota(I32, s.shape, 0)
    vals, picks = [], []
    for _ in range(k):
        m = jnp.max(s, axis=0, keepdims=True)
        idx = jnp.min(jnp.where(s == m, tag, _INT_MAX), axis=0, keepdims=True)
        vals.append(m)
        picks.append(idx)
        s = jnp.where(tag == idx, -jnp.inf, s)
    return jnp.concatenate(vals, axis=0), jnp.concatenate(picks, axis=0)


def _candidates(v1, v2):
    k, tm = v1.shape
    row8 = lax.broadcasted_iota(I32, (SUBLANES, tm), 0)
    row16 = lax.broadcasted_iota(I32, (k, tm), 0)
    sums = [v1[0:1] + v2, v1[1:2] + v2[0:SUBLANES], v1[SUBLANES:] + v2[0:1]]
    tags = [row16, k + row8, (row8 + SUBLANES) * k]
    mid = 3 * SUBLANES
    rowm = lax.broadcasted_iota(I32, (mid, tm), 0)
    arow = jnp.full((mid, tm), -1, I32)
    brow = jnp.zeros((mid, tm), I32)
    start = 0
    for a in range(2, SUBLANES):
        nb = k // (a + 1)
        inrun = (rowm >= start) & (rowm < start + nb)
        arow = jnp.where(inrun, a, arow)
        brow = jnp.where(inrun, rowm - start, brow)
        start += nb
    part_a = jnp.zeros((mid, tm), F32)
    part_b = jnp.zeros((mid, tm), F32)
    for a in range(2, SUBLANES):
        part_a = jnp.where(arow == a, v1[a:a + 1], part_a)
    for b in range(k // 3):
        part_b = jnp.where(brow == b, v2[b:b + 1], part_b)
    used = arow >= 0
    sums.append(jnp.where(used, part_a + part_b, -jnp.inf))
    tags.append(jnp.where(used, arow * k + brow, _INT_MAX))
    return jnp.concatenate(sums, axis=0), jnp.concatenate(tags, axis=0)


def _take_rows(table, idx):
    out = jnp.zeros(idx.shape, table.dtype)
    for a in range(table.shape[0]):
        out = jnp.where(idx == a, table[a:a + 1], out)
    return out


def _peer_topk_kernel(q_ref, keys_ref, id_out, gate_out):
    half = PEER_DKEY // 2
    dn = (((1,), (1,)), ((), ()))
    q = q_ref[...]
    s1 = lax.dot_general(keys_ref[0], q[:, :half], dn, preferred_element_type=F32, precision=lax.Precision.HIGHEST)
    s2 = lax.dot_general(keys_ref[1], q[:, half:], dn, preferred_element_type=F32, precision=lax.Precision.HIGHEST)
    v1, i1 = _top_k_rows(s1, PEER_TOPK)
    v2, i2 = _top_k_rows(s2, PEER_TOPK)
    cand, flat = _candidates(v1, v2)
    sc, pick = _top_k_rows(cand, PEER_TOPK, tag=flat)
    e = jnp.exp(sc - sc[0:1, :])
    rank1, rank2 = pick >> 4, pick & (PEER_TOPK - 1)
    id_out[0] = _take_rows(i1, rank1) * PEER_NKEYS + _take_rows(i2, rank2)
    gate_out[0] = e / jnp.sum(e, axis=0, keepdims=True)


def _peer_topk(q, subkeys, tm=256):
    t = q.shape[0]
    out = pl.BlockSpec((1, PEER_TOPK, tm), lambda i, h: (h, 0, i))
    return pl.pallas_call(
        _peer_topk_kernel, name="peer_topk",
        grid=(t // tm, PEER_HEADS),
        in_specs=[pl.BlockSpec((tm, PEER_DKEY), lambda i, h: (i, h)),
                  pl.BlockSpec(subkeys.shape, lambda i, h: (0, 0, 0))],
        out_specs=[out, out],
        out_shape=[jax.ShapeDtypeStruct((PEER_HEADS, PEER_TOPK, t), I32),
                   jax.ShapeDtypeStruct((PEER_HEADS, PEER_TOPK, t), F32)],
        compiler_params=_cparams("parallel", "parallel"),
    )(q, subkeys)


def _fold_sublanes(vals, masks):
    for level, mask in enumerate(masks):
        nxt = []
        for a, b in zip(vals[0::2], vals[1::2]):
            nxt.append(jnp.where(mask, a, b) + pltpu.roll(jnp.where(mask, b, a), 1 << level, axis=0))
        vals = nxt
    (out,) = vals
    return out


def _pack_table(w):
    bits = lax.bitcast_convert_type(w.astype(BF16), jnp.uint16).astype(jnp.uint32)
    half = D_MODEL // 2
    words = lax.bitcast_convert_type((bits[:, half:] << 16) | bits[:, :half], I32)
    return jnp.pad(words.reshape(-1, LANES), ((0, SUBLANES), (0, 0)))


def _unpack(w):
    return pltpu.bitcast(w << 16, F32), pltpu.bitcast(w & -0x10000, F32)


def _peer_hidden_kernel(id_ref, x_ref, u_ref, h_out, r_scr):
    tm = x_ref.shape[0]
    sub = lax.broadcasted_iota(I32, (SUBLANES, LANES), 0)
    sub_masks = [(sub % (2 << lv)) < (1 << lv) for lv in range(3)]
    groups = PEER_SLOTS // SUBLANES
    ones = jnp.ones((SUBLANES, LANES), BF16)
    dn = (((1,), (1,)), ((), ()))

    def token(tt, g):
        t = g * SUBLANES + tt
        xt = x_ref[t]
        x_lo = jnp.where(sub < PACK_ROWS, xt, 0.0)
        x_hi = jnp.where(sub < PACK_ROWS, pltpu.roll(xt, PACK_ROWS, axis=0), 0.0)
        for j in range(groups):
            prods = []
            for s in range(SUBLANES):
                row = pl.multiple_of(id_ref[t * PEER_SLOTS + j * SUBLANES + s], PACK_ROWS)
                lo, hi = _unpack(u_ref[pl.ds(row, SUBLANES), :])
                prods.append(lo * x_lo + hi * x_hi)
            r_scr[pl.ds(pl.multiple_of(tt * PEER_SLOTS + j * SUBLANES, SUBLANES), SUBLANES), :] = _fold_sublanes(prods, sub_masks)
        return g

    def group(g, _):
        lax.fori_loop(0, SUBLANES, token, g)
        r = r_scr[...]
        hi = r.astype(BF16)
        lo = (r - hi.astype(F32)).astype(BF16)
        sums = (lax.dot_general(ones, hi, dn, preferred_element_type=F32)
                + lax.dot_general(ones, lo, dn, preferred_element_type=F32))
        out = jnp.zeros((SUBLANES, LANES), F32)
        for tt in range(SUBLANES):
            out = jnp.where(sub == tt, sums[:, tt * PEER_SLOTS:(tt + 1) * PEER_SLOTS], out)
        h_out[pl.ds(pl.multiple_of(g * SUBLANES, SUBLANES), SUBLANES), :] = out
        return 0

    lax.fori_loop(0, tm // SUBLANES, group, 0)


def _smem_rows(tm):
    return pl.BlockSpec((tm * PEER_SLOTS,), lambda i: (i,), memory_space=pltpu.SMEM)


def _table_spec():
    return pl.BlockSpec((TABLE_ROWS, LANES), lambda i: (0, 0), pipeline_mode=pl.Buffered(1))


def _peer_hidden(rows, x3, u_packed, tm=256):
    t = x3.shape[0]
    return pl.pallas_call(
        _peer_hidden_kernel, name="peer_hidden",
        grid=(t // tm,),
        in_specs=[_smem_rows(tm), pl.BlockSpec((tm, SUBLANES, LANES), lambda i: (i, 0, 0)), _table_spec()],
        out_specs=pl.BlockSpec((tm, PEER_SLOTS), lambda i: (i, 0)),
        out_shape=jax.ShapeDtypeStruct((t, PEER_SLOTS), F32),
        scratch_shapes=[pltpu.VMEM((SUBLANES * PEER_SLOTS, LANES), F32)],
        compiler_params=_cparams("arbitrary"),
    )(rows, x3, u_packed)


def _peer_coef_kernel(h_ref, gate_ref, c_out):
    ct = (gate_ref[...] * _gelu(h_ref[...])).T
    for g in range(ct.shape[1] // SUBLANES):
        c_out[g] = ct if g == 0 else pltpu.roll(ct, LANES - SUBLANES * g, axis=1)


def _peer_coef(hid, gate, tm=LANES):
    t = gate.shape[0]
    return pl.pallas_call(
        _peer_coef_kernel, name="peer_coef",
        grid=(t // tm,),
        in_specs=[pl.BlockSpec((tm, PEER_SLOTS), lambda i: (i, 0)), pl.BlockSpec((tm, PEER_SLOTS), lambda i: (i, 0))],
        out_specs=pl.BlockSpec((tm // SUBLANES, PEER_SLOTS, LANES), lambda i: (i, 0, 0)),
        out_shape=jax.ShapeDtypeStruct((t // SUBLANES, PEER_SLOTS, LANES), F32),
        compiler_params=_cparams("parallel"),
    )(hid, gate)


def _peer_value_kernel(id_ref, c_ref, v_ref, y_out, cb_even, cb_odd):
    tm = y_out.shape[0]
    n_groups = tm // SUBLANES
    sub = lax.broadcasted_iota(I32, (SUBLANES, LANES), 0)

    def spread(cb, g, e0, ne):
        rows = c_ref[g, pl.ds(e0, ne), :]
        for n in range(SUBLANES):
            cb[pl.ds(n * PEER_SLOTS + e0, ne), :] = jnp.broadcast_to(rows[:, n:n + 1], (ne, LANES))

    def run_group(g, cb_use, cb_fill):
        g_next = jnp.minimum(g + 1, n_groups - 1)
        per_token = PEER_SLOTS // SUBLANES

        def token(tt, _):
            t = g * SUBLANES + tt
            spread(cb_fill, g_next, pl.multiple_of(tt * per_token, per_token), per_token)
            cb_scr = cb_use

            acc_lo = jnp.zeros((SUBLANES, LANES), F32)
            acc_hi = jnp.zeros((SUBLANES, LANES), F32)
            for j in range(PEER_SLOTS // SUBLANES):
                t_lo, t_hi = [], []
                for s in range(SUBLANES):
                    e = j * SUBLANES + s
                    row = pl.multiple_of(id_ref[t * PEER_SLOTS + e], PACK_ROWS)
                    c = jnp.broadcast_to(cb_scr[pl.ds(tt * PEER_SLOTS + e, 1), :], (SUBLANES, LANES))
                    lo, hi = _unpack(v_ref[pl.ds(row, SUBLANES), :])
                    t_lo.append(c * lo)
                    t_hi.append(c * hi)
                while len(t_lo) > 1:
                    t_lo = [a + b for a, b in zip(t_lo[0::2], t_lo[1::2])]
                    t_hi = [a + b for a, b in zip(t_hi[0::2], t_hi[1::2])]
                acc_lo = acc_lo + t_lo[0]
                acc_hi = acc_hi + t_hi[0]
            y_out[t] = jnp.where(sub < PACK_ROWS, acc_lo, pltpu.roll(acc_hi, PACK_ROWS, axis=0))
            return 0

        lax.fori_loop(0, SUBLANES, token, 0)

    def pair(k, _):
        run_group(2 * k, cb_even, cb_odd)
        run_group(2 * k + 1, cb_odd, cb_even)
        return 0

    spread(cb_even, 0, 0, PEER_SLOTS)
    lax.fori_loop(0, n_groups // 2, pair, 0)


def _peer_value(rows, coef, v_packed, tm=256):
    t = coef.shape[0] * SUBLANES
    return pl.pallas_call(
        _peer_value_kernel, name="peer_value",
        grid=(t // tm,),
        in_specs=[_smem_rows(tm), pl.BlockSpec((tm // SUBLANES, PEER_SLOTS, LANES), lambda i: (i, 0, 0)), _table_spec()],
        out_specs=pl.BlockSpec((tm, SUBLANES, LANES), lambda i: (i, 0, 0)),
        out_shape=jax.ShapeDtypeStruct((t, SUBLANES, LANES), F32),
        scratch_shapes=[pltpu.VMEM((SUBLANES * PEER_SLOTS, LANES), F32)] * 2,
        compiler_params=_cparams("arbitrary"),
    )(rows, coef, v_packed)


def _residual_kernel(x_ref, y_ref, g_ref, o_ref, *, final):
    x = x_ref[...] + y_ref[...]
    o_ref[...] = _rmsnorm(x, g_ref[...], D_MODEL) if final else x


def _residual(x1, y, gain, final, tm=512):
    t = x1.shape[0]
    return pl.pallas_call(
        functools.partial(_residual_kernel, final=final), name="peer_residual",
        grid=(t // tm,),
        in_specs=[pl.BlockSpec((tm, D_MODEL), lambda i: (i, 0)),
                  pl.BlockSpec((tm, D_MODEL), lambda i: (i, 0)),
                  pl.BlockSpec((1, D_MODEL), lambda i: (0, 0))],
        out_specs=pl.BlockSpec((tm, D_MODEL), lambda i: (i, 0)),
        out_shape=jax.ShapeDtypeStruct((t, D_MODEL), F32),
        compiler_params=_cparams("parallel"),
    )(x1, y, gain)


def _peer_ffn(x1, xn, q, subkeys, u, v, gain, final):
    t = x1.shape[0]
    ids_h, gate_h = _peer_topk(q, subkeys)
    ids = ids_h.reshape(PEER_SLOTS, t).T
    gate = gate_h.reshape(PEER_SLOTS, t).T
    rows = (ids * PACK_ROWS).reshape(-1)
    hid = _peer_hidden(rows, xn.reshape(t, SUBLANES, LANES), _pack_table(u))
    coef = _peer_coef(hid, gate)
    y = _peer_value(rows, coef, _pack_table(v))
    return _residual(x1, y.reshape(t, D_MODEL), gain, final)


def kernel(x, positions, norm_mix, w_in, q_norm, w_uq, kv_norm, w_ukv, conv_w, conv_b, lru_wa, lru_ba, lru_wx, lru_bx,
           lru_lambda, swa_sink, grp_norm, w_out, norm_ffn, peer_wq, peer_subkeys, peer_u, peer_v, norm_final):
    batch, seq, _ = x.shape
    t = batch * seq
    depth = w_in.shape[0]
    xt = x.reshape(t, D_MODEL)

    pos = positions.astype(F32).reshape(t, 1)
    zeros = lambda n: jnp.zeros((n,), F32)
    inv_m, inv_s = _inv_freq(MLA_ROPE), _inv_freq(SWA_HEAD_DIM)
    inv = jnp.concatenate([zeros(MLA_NOPE), inv_m, inv_m, zeros(HP - MLA_NOPE - MLA_ROPE),
                           inv_s, inv_s, zeros(HP - SWA_HEAD_DIM)])
    cos, sin = _rope_tables(pos, inv[None, :])

    for l in range(depth):
        w_in_p, wq, wkv, w_gate, b_gate, ow = _layer_weights(
            l, w_in, w_uq, w_ukv, lru_wa, lru_ba, lru_wx, lru_bx, grp_norm, w_out)
        q, k, v, sq, sk, sv, lx, lg = _in_proj(
            xt, norm_mix[l][None, :], w_in_p, q_norm[l][None, :], wq, kv_norm[l][None, :], wkv, cos, sin)
        o_mla = _mla_attention(q, k, v, batch, seq)
        o_swa = _swa_attention(sq, sk, sv, swa_sink[l][None, :], batch, seq)
        h_f, h_b = _lru_scan(lx, conv_w[l], conv_b[l][None, :], w_gate, b_gate, lru_lambda[l], batch, seq)
        x1, xn, pq = _out_proj(xt, o_mla, h_f, h_b, lg, o_swa, ow, norm_ffn[l][None, :], peer_wq[l].astype(BF16))
        final = l == depth - 1
        gain = norm_final[None, :] if final else jnp.ones((1, D_MODEL), F32)
        xt = _peer_ffn(x1, xn, pq, peer_subkeys[l], peer_u[l], peer_v[l], gain, final)
    return xt.reshape(batch, seq, D_MODEL)
```

```python
import functools

import jax
import jax.numpy as jnp
from jax import lax
from jax.experimental import pallas as pl
from jax.experimental.pallas import tpu as pltpu

F32 = jnp.float32
BF16 = jnp.bfloat16
I32 = jnp.int32

D_MODEL = 1024
MLA_HEADS, MLA_NOPE, MLA_ROPE, MLA_V = 6, 64, 32, 64
MLA_Q_RANK, MLA_KV_RANK = 256, 128
LRU_WIDTH, LRU_BLOCKS, LRU_C = 384, 6, 8.0
SWA_Q_HEADS, SWA_KV_HEADS, SWA_HEAD_DIM, SWA_WINDOW = 4, 2, 64, 128
ROPE_THETA, EPS, NEG_BIG = 10000.0, 1e-6, -1e30
PEER_HEADS, PEER_NKEYS, PEER_DKEY, PEER_TOPK = 8, 128, 256, 16
PEER_SLOTS = PEER_HEADS * PEER_TOPK
PEER_EXPERTS = PEER_NKEYS * PEER_NKEYS

LANES = 128
SUBLANES = 8
HP = LANES
MLA_QK = MLA_HEADS * HP
SWA_Q = SWA_Q_HEADS * HP
SWA_KV = SWA_KV_HEADS * HP
VMEM_LIMIT = 56 * 1024 * 1024
PACK_ROWS = D_MODEL // (2 * LANES)
TABLE_ROWS = PEER_EXPERTS * PACK_ROWS + 2 * PACK_ROWS

_C_Q, _C_KV, _S_Q, _S_QR, _S_K, _S_KR, _S_V, _L_X, _L_G, _K_R, _IN_COLS = (
    0, 256, 384, 896, 1408, 1664, 1920, 2176, 2560, 2944, 3072)


def _cparams(*sem):
    return pltpu.CompilerParams(dimension_semantics=sem, vmem_limit_bytes=VMEM_LIMIT)


def _rmsnorm(x, g, n):
    return x * lax.rsqrt(jnp.sum(x * x, axis=-1, keepdims=True) * (1.0 / n) + EPS) * g


def _sigmoid(x):
    return 1.0 / (1.0 + jnp.exp(-x))


def _gelu(x):
    return 0.5 * x * (1.0 + lax.erf(x * (2.0 ** -0.5)))


def _rot_half_cols(w, d):
    k = w.shape[0]
    w3 = w.reshape(k, -1, d)
    return jnp.concatenate([-w3[..., d // 2:], w3[..., : d // 2]], axis=-1).reshape(k, -1)


def _pad_heads(w, d, lead=0):
    k = w.shape[0]
    w3 = w.reshape(k, -1, d)
    return jnp.pad(w3, ((0, 0), (0, 0), (lead, HP - d - lead))).reshape(k, -1)


def _inv_freq(d):
    return ROPE_THETA ** (-jnp.arange(0, d, 2, dtype=F32) / d)


def _layer_weights(l, w_in, w_uq, w_ukv, lru_wa, lru_ba, lru_wx, lru_bx, grp_norm, w_out):
    wi = w_in[l]
    c_q, c_kv, k_r = wi[:, 0:256], wi[:, 256:384], wi[:, 384:416]
    s_q, s_k, s_v = wi[:, 416:672], wi[:, 672:800], wi[:, 800:928]
    l_x, l_g = wi[:, 928:1312], wi[:, 1312:1696]
    k_r_blk = jnp.concatenate([jnp.zeros((D_MODEL, MLA_NOPE), F32), k_r, _rot_half_cols(k_r, MLA_ROPE)], axis=1)
    w_in_p = jnp.concatenate([
        c_q, c_kv,
        _pad_heads(s_q, SWA_HEAD_DIM), _pad_heads(_rot_half_cols(s_q, SWA_HEAD_DIM), SWA_HEAD_DIM),
        _pad_heads(s_k, SWA_HEAD_DIM), _pad_heads(_rot_half_cols(s_k, SWA_HEAD_DIM), SWA_HEAD_DIM),
        _pad_heads(s_v, SWA_HEAD_DIM), l_x, l_g, k_r_blk], axis=1).astype(BF16)

    wq3 = w_uq[l].reshape(MLA_Q_RANK, MLA_HEADS, MLA_NOPE + MLA_ROPE)
    q_nope = wq3[..., :MLA_NOPE].reshape(MLA_Q_RANK, -1)
    q_rope = wq3[..., MLA_NOPE:].reshape(MLA_Q_RANK, -1)
    wq = jnp.concatenate([
        _pad_heads(q_nope, MLA_NOPE) + _pad_heads(q_rope, MLA_ROPE, lead=MLA_NOPE),
        _pad_heads(_rot_half_cols(q_rope, MLA_ROPE), MLA_ROPE, lead=MLA_NOPE)], axis=1).astype(BF16)

    wkv3 = w_ukv[l].reshape(MLA_KV_RANK, MLA_HEADS, MLA_NOPE + MLA_V)
    wkv = jnp.concatenate([
        _pad_heads(wkv3[..., :MLA_NOPE].reshape(MLA_KV_RANK, -1), MLA_NOPE),
        _pad_heads(wkv3[..., MLA_NOPE:].reshape(MLA_KV_RANK, -1), MLA_V)], axis=1).astype(BF16)

    def blockdiag(w):
        n, bi, bj = w.shape
        eye = jnp.eye(n, dtype=w.dtype)
        return (w[:, :, None, :] * eye[:, None, :, None]).reshape(n * bi, n * bj)

    w_gate = [jnp.concatenate([blockdiag(lru_wa[l, d]), blockdiag(lru_wx[l, d])], axis=1).astype(BF16) for d in range(2)]
    b_gate = [jnp.concatenate([lru_ba[l, d].reshape(1, -1), lru_bx[l, d].reshape(1, -1)], axis=1) for d in range(2)]

    g = grp_norm[l]
    wo = w_out[l]
    ga, gb = MLA_HEADS * MLA_V, MLA_HEADS * MLA_V + LRU_WIDTH

    def pad_rows(w, d):
        return jnp.pad(w.reshape(-1, d, w.shape[-1]), ((0, 0), (0, HP - d), (0, 0))).reshape(-1, w.shape[-1])

    out_w = dict(
        g_a=_pad_heads(g[None, :ga], MLA_V), g_b=g[None, ga:gb], g_c=_pad_heads(g[None, gb:], SWA_HEAD_DIM),
        w_a=pad_rows(wo[:ga], MLA_V).astype(BF16), w_b=wo[ga:gb].astype(BF16),
        w_c=pad_rows(wo[gb:], SWA_HEAD_DIM).astype(BF16))
    return w_in_p, wq, wkv, w_gate, b_gate, out_w


def _rope_table_kernel(pos_ref, inv_ref, cos_ref, sin_ref):
    ang = pos_ref[...] * inv_ref[...]
    cos_ref[...] = jnp.cos(ang)
    sin_ref[...] = jnp.sin(ang)


def _rope_tables(pos, inv, tm=512):
    t, w = pos.shape[0], inv.shape[1]
    return pl.pallas_call(
        _rope_table_kernel, name="rope_tables",
        grid=(t // tm,),
        in_specs=[pl.BlockSpec((tm, 1), lambda i: (i, 0)), pl.BlockSpec((1, w), lambda i: (0, 0))],
        out_specs=[pl.BlockSpec((tm, w), lambda i: (i, 0))] * 2,
        out_shape=[jax.ShapeDtypeStruct((t, w), F32)] * 2,
        compiler_params=_cparams("parallel"),
    )(pos, inv)


def _in_proj_kernel(x_ref, nmix_ref, win_ref, qn_ref, wq_ref, kvn_ref, wkv_ref, cos_ref, sin_ref,
                    q_out, k_out, v_out, sq_out, sk_out, sv_out, lx_out, lg_out):
    xn = _rmsnorm(x_ref[...], nmix_ref[...], D_MODEL)
    proj = jnp.dot(xn.astype(BF16), win_ref[...], preferred_element_type=F32)
    cos, sin = cos_ref[...], sin_ref[...]
    cosq = jnp.concatenate([cos[:, :HP]] * MLA_HEADS, axis=1)
    sinq = jnp.concatenate([sin[:, :HP]] * MLA_HEADS, axis=1)
    coss = jnp.concatenate([cos[:, HP:]] * SWA_Q_HEADS, axis=1)
    sins = jnp.concatenate([sin[:, HP:]] * SWA_Q_HEADS, axis=1)

    swa_scale = SWA_HEAD_DIM ** -0.5
    sq_out[...] = ((proj[:, _S_Q:_S_QR] * coss + proj[:, _S_QR:_S_K] * sins) * swa_scale).astype(BF16)
    sk_out[...] = (proj[:, _S_K:_S_KR] * coss[:, :SWA_KV] + proj[:, _S_KR:_S_V] * sins[:, :SWA_KV]).astype(BF16)
    sv_out[...] = proj[:, _S_V:_L_X].astype(BF16)
    lx_out[...] = proj[:, _L_X:_L_G]
    lg_out[...] = proj[:, _L_G:_K_R]

    cqn = _rmsnorm(proj[:, _C_Q:_C_KV], qn_ref[...], MLA_Q_RANK).astype(BF16)
    qq = jnp.dot(cqn, wq_ref[...], preferred_element_type=F32)
    mla_scale = (MLA_NOPE + MLA_ROPE) ** -0.5
    q_out[...] = ((qq[:, :MLA_QK] * cosq + qq[:, MLA_QK:] * sinq) * mla_scale).astype(BF16)

    kvn = _rmsnorm(proj[:, _C_KV:_S_Q], kvn_ref[...], MLA_KV_RANK).astype(BF16)
    kv = jnp.dot(kvn, wkv_ref[...], preferred_element_type=F32)
    kr = proj[:, _K_R:_IN_COLS]
    lane = lax.broadcasted_iota(I32, kr.shape, 1)
    in_rope = (lane >= MLA_NOPE) & (lane < MLA_NOPE + MLA_ROPE)
    kr_rot = jnp.where(in_rope, kr * cosq[:, :HP] + pltpu.roll(kr, HP - MLA_ROPE, axis=1) * sinq[:, :HP], 0.0)
    for h in range(MLA_HEADS):
        k_out[:, h * HP:(h + 1) * HP] = (kv[:, h * HP:(h + 1) * HP] + kr_rot).astype(BF16)
    vlane = lax.broadcasted_iota(I32, (kv.shape[0], MLA_QK), 1)
    v_out[...] = jnp.where(vlane % HP == MLA_V, 1.0, kv[:, MLA_QK:]).astype(BF16)


def _in_proj(x, nmix, w_in_p, qn, wq, kvn, wkv, cos, sin, tm=256):
    t = x.shape[0]
    row = lambda w: pl.BlockSpec((tm, w), lambda i: (i, 0))
    full = lambda a: pl.BlockSpec(a.shape, lambda i: (0,) * a.ndim)
    widths = (MLA_QK, MLA_QK, MLA_QK, SWA_Q, SWA_KV, SWA_KV, LRU_WIDTH, LRU_WIDTH)
    dtypes = (BF16, BF16, BF16, BF16, BF16, BF16, F32, F32)
    return pl.pallas_call(
        _in_proj_kernel, name="in_proj",
        grid=(t // tm,),
        in_specs=[row(D_MODEL), full(nmix), full(w_in_p), full(qn), full(wq), full(kvn), full(wkv),
                  row(2 * HP), row(2 * HP)],
        out_specs=[row(w) for w in widths],
        out_shape=[jax.ShapeDtypeStruct((t, w), d) for w, d in zip(widths, dtypes)],
        compiler_params=_cparams("parallel"),
    )(x, nmix, w_in_p, qn, wq, kvn, wkv, cos, sin)


def _mla_kernel(q_ref, k_ref, v_ref, o_ref, *, tk):
    q = q_ref[...]
    tq = q.shape[0]
    nk = k_ref.shape[0] // tk

    def body(j, carry):
        m, acc = carry
        off = pl.multiple_of(j * tk, tk)
        k = k_ref[pl.ds(off, tk), :]
        v = v_ref[pl.ds(off, tk), :]
        s = lax.dot_general(q, k, (((1,), (1,)), ((), ())), preferred_element_type=F32)
        m_new = jnp.maximum(m, jnp.max(s, axis=-1, keepdims=True))
        p = jnp.exp(s - m_new)
        acc = jnp.exp(m - m_new) * acc + jnp.dot(p.astype(BF16), v, preferred_element_type=F32)
        return m_new, acc

    init = (jnp.full((tq, 1), -jnp.inf, F32), jnp.zeros((tq, HP), F32))
    _, acc = lax.fori_loop(0, nk, body, init)
    lane = lax.broadcasted_iota(I32, acc.shape, 1)
    o_ref[...] = jnp.where(lane < MLA_V, acc / acc[:, MLA_V:MLA_V + 1], 0.0)


def _mla_attention(q, k, v, batch, seq, tq=2048, tk=1024):
    tq, tk = min(tq, seq), min(tk, seq)
    nq = seq // tq
    return pl.pallas_call(
        functools.partial(_mla_kernel, tk=tk), name="mla_attention",
        grid=(batch, MLA_HEADS, nq),
        in_specs=[pl.BlockSpec((tq, HP), lambda b, h, i: (b * nq + i, h)),
                  pl.BlockSpec((seq, HP), lambda b, h, i: (b, h)),
                  pl.BlockSpec((seq, HP), lambda b, h, i: (b, h))],
        out_specs=pl.BlockSpec((tq, HP), lambda b, h, i: (b * nq + i, h)),
        out_shape=jax.ShapeDtypeStruct((batch * seq, MLA_QK), F32),
        compiler_params=_cparams("parallel", "parallel", "arbitrary"),
    )(q, k, v)


def _swa_kernel(sink_ref, q_ref, kp_ref, kc_ref, kn_ref, vp_ref, vc_ref, vn_ref, o_ref, *, seq):
    w = SWA_WINDOW
    tq = q_ref.shape[0]
    i = pl.program_id(1)
    kcat = jnp.concatenate([kp_ref[...], kc_ref[...], kn_ref[...]], axis=0)
    vcat = jnp.concatenate([vp_ref[...], vc_ref[...], vn_ref[...]], axis=0)
    for j in range(tq // w):
        qpos = i * tq + j * w + lax.broadcasted_iota(I32, (w, 3 * w), 0)
        kpos = i * tq + (j - 1) * w + lax.broadcasted_iota(I32, (w, 3 * w), 1)
        valid = (jnp.abs(kpos - qpos) <= w) & (kpos >= 0) & (kpos < seq)
        for h in range(SWA_Q_HEADS):
            kh = h // (SWA_Q_HEADS // SWA_KV_HEADS)
            qh = q_ref[j * w:(j + 1) * w, h * HP:(h + 1) * HP]
            kj = kcat[j * w:(j + 3) * w, kh * HP:(kh + 1) * HP]
            vj = vcat[j * w:(j + 3) * w, kh * HP:(kh + 1) * HP]
            s = lax.dot_general(qh, kj, (((1,), (1,)), ((), ())), preferred_element_type=F32)
            s = jnp.where(valid, s, NEG_BIG)
            sink = sink_ref[0, h]
            m = jnp.maximum(jnp.max(s, axis=-1, keepdims=True), sink)
            p = jnp.exp(s - m)
            den = jnp.sum(p, axis=-1, keepdims=True) + jnp.exp(sink - m)
            o_ref[j * w:(j + 1) * w, h * HP:(h + 1) * HP] = jnp.dot(p.astype(BF16), vj, preferred_element_type=F32) / den


def _swa_attention(sq, sk, sv, sink, batch, seq, tq=512):
    w = SWA_WINDOW
    nq, nw, r = seq // tq, seq // w, tq // w
    prev = pl.BlockSpec((w, SWA_KV), lambda b, i: (b * nw + jnp.maximum(i * r - 1, 0), 0))
    cur = pl.BlockSpec((tq, SWA_KV), lambda b, i: (b * nq + i, 0))
    nxt = pl.BlockSpec((w, SWA_KV), lambda b, i: (b * nw + jnp.minimum((i + 1) * r, nw - 1), 0))
    return pl.pallas_call(
        functools.partial(_swa_kernel, seq=seq), name="swa_attention",
        grid=(batch, nq),
        in_specs=[pl.BlockSpec(memory_space=pltpu.SMEM),
                  pl.BlockSpec((tq, SWA_Q), lambda b, i: (b * nq + i, 0)),
                  prev, cur, nxt, prev, cur, nxt],
        out_specs=pl.BlockSpec((tq, SWA_Q), lambda b, i: (b * nq + i, 0)),
        out_shape=jax.ShapeDtypeStruct((batch * seq, SWA_Q), F32),
        compiler_params=_cparams("parallel", "parallel"),
    )(sink, sq, sk, sk, sk, sv, sv, sv)


def _lru_kernel(xpf_ref, xcf_ref, xnf_ref, xpb_ref, xcb_ref, xnb_ref, cw_ref, cb_ref, wf_ref, bf_ref, wb_ref, bb_ref,
                lam_ref, hf_out, hb_out, carry_f, carry_b, *, nt):
    tm = xcf_ref.shape[0]
    i = pl.program_id(1)
    halo = xpf_ref.shape[0]

    @pl.when(i == 0)
    def _():
        carry_f[...] = jnp.zeros_like(carry_f)
        carry_b[...] = jnp.zeros_like(carry_b)

    def gates(xp_ref, xc_ref, xn_ref, first, last, w_ref, b_ref, lam):
        prev = jnp.where(first, 0.0, xp_ref[...])
        nxt = jnp.where(last, 0.0, xn_ref[...])
        xcat = jnp.concatenate([prev, xc_ref[...], nxt], axis=0)
        cw = cw_ref[...]
        conv = cb_ref[...]
        for tap in range(cw.shape[0]):
            conv = conv + cw[tap:tap + 1, :] * xcat[halo - 1 + tap: halo - 1 + tap + tm, :]
        g = jnp.dot(conv.astype(BF16), w_ref[...], preferred_element_type=F32) + b_ref[...]
        r = _sigmoid(g[:, :LRU_WIDTH])
        gate_i = _sigmoid(g[:, LRU_WIDTH:])
        softplus = jnp.maximum(-lam, 0.0) + jnp.log1p(jnp.exp(-jnp.abs(lam)))
        log_a = -LRU_C * r * softplus
        a = jnp.exp(log_a)
        b = jnp.sqrt(1.0 - a * a) * (gate_i * conv)
        return a, b

    row = lax.broadcasted_iota(I32, (tm, LRU_WIDTH), 0)

    def scan(a, b, reverse):
        k = 1
        while k < tm:
            if reverse:
                keep = row < tm - k
                shift = tm - k
            else:
                keep = row >= k
                shift = k
            a_s = jnp.where(keep, pltpu.roll(a, shift, axis=0), 1.0)
            b_s = jnp.where(keep, pltpu.roll(b, shift, axis=0), 0.0)
            b = a * b_s + b
            a = a * a_s
            k *= 2
        return a, b

    a, b = gates(xpf_ref, xcf_ref, xnf_ref, i == 0, i == nt - 1, wf_ref, bf_ref, lam_ref[0:1, :])
    a, b = scan(a, b, False)
    h = a * carry_f[...] + b
    hf_out[...] = h
    carry_f[...] = h[tm - 1:tm, :]

    a, b = gates(xpb_ref, xcb_ref, xnb_ref, i == nt - 1, i == 0, wb_ref, bb_ref, lam_ref[1:2, :])
    a, b = scan(a, b, True)
    h = a * carry_b[...] + b
    hb_out[...] = h
    carry_b[...] = h[0:1, :]


def _lru_scan(lx, conv_w, conv_b, w_gate, b_gate, lam, batch, seq, tm=256):
    nt, hb = seq // tm, seq // SUBLANES
    r = tm // SUBLANES
    fwd = lambda b, i: i
    bwd = lambda b, i: nt - 1 - i

    def specs(tile):
        return [pl.BlockSpec((SUBLANES, LRU_WIDTH), lambda b, i: (b * hb + jnp.maximum(tile(b, i) * r - 1, 0), 0)),
                pl.BlockSpec((tm, LRU_WIDTH), lambda b, i: (b * nt + tile(b, i), 0)),
                pl.BlockSpec((SUBLANES, LRU_WIDTH), lambda b, i: (b * hb + jnp.minimum((tile(b, i) + 1) * r, hb - 1), 0))]

    full = lambda a: pl.BlockSpec(a.shape, lambda b, i: (0,) * a.ndim)
    return pl.pallas_call(
        functools.partial(_lru_kernel, nt=nt), name="lru_scan",
        grid=(batch, nt),
        in_specs=specs(fwd) + specs(bwd) + [full(conv_w), full(conv_b), full(w_gate[0]), full(b_gate[0]),
                                            full(w_gate[1]), full(b_gate[1]), full(lam)],
        out_specs=[pl.BlockSpec((tm, LRU_WIDTH), lambda b, i: (b * nt + i, 0)),
                   pl.BlockSpec((tm, LRU_WIDTH), lambda b, i: (b * nt + nt - 1 - i, 0))],
        out_shape=[jax.ShapeDtypeStruct((batch * seq, LRU_WIDTH), F32)] * 2,
        scratch_shapes=[pltpu.VMEM((1, LRU_WIDTH), F32), pltpu.VMEM((1, LRU_WIDTH), F32)],
        compiler_params=_cparams("parallel", "arbitrary"),
    )(lx, lx, lx, lx, lx, lx, conv_w, conv_b, w_gate[0], b_gate[0], w_gate[1], b_gate[1], lam)


def _out_proj_kernel(x_ref, oa_ref, hf_ref, hb_ref, lg_ref, oc_ref, ga_ref, gb_ref, gc_ref, wa_ref, wb_ref, wc_ref,
                     nffn_ref, wq_ref, x1_out, xn_out, q_out):
    mix_a = _rmsnorm(oa_ref[...], ga_ref[...], MLA_HEADS * MLA_V).astype(BF16)
    o_lru = (hf_ref[...] + hb_ref[...]) * _gelu(lg_ref[...])
    mix_b = _rmsnorm(o_lru, gb_ref[...], LRU_WIDTH).astype(BF16)
    mix_c = _rmsnorm(oc_ref[...], gc_ref[...], SWA_Q_HEADS * SWA_HEAD_DIM).astype(BF16)
    x1 = (x_ref[...]
          + jnp.dot(mix_a, wa_ref[...], preferred_element_type=F32)
          + jnp.dot(mix_b, wb_ref[...], preferred_element_type=F32)
          + jnp.dot(mix_c, wc_ref[...], preferred_element_type=F32))
    x1_out[...] = x1
    xn = _rmsnorm(x1, nffn_ref[...], D_MODEL)
    xn_out[...] = xn
    q_out[...] = jnp.dot(xn.astype(BF16), wq_ref[...], preferred_element_type=F32)


def _out_proj(x, o_mla, h_f, h_b, l_g, o_swa, ow, nffn, wq, tm=256):
    t = x.shape[0]
    row = lambda w: pl.BlockSpec((tm, w), lambda i: (i, 0))
    full = lambda a: pl.BlockSpec(a.shape, lambda i: (0,) * a.ndim)
    nq = wq.shape[1]
    return pl.pallas_call(
        _out_proj_kernel, name="out_proj",
        grid=(t // tm,),
        in_specs=[row(D_MODEL), row(MLA_QK), row(LRU_WIDTH), row(LRU_WIDTH), row(LRU_WIDTH), row(SWA_Q),
                  full(ow["g_a"]), full(ow["g_b"]), full(ow["g_c"]), full(ow["w_a"]), full(ow["w_b"]), full(ow["w_c"]),
                  full(nffn), full(wq)],
        out_specs=[row(D_MODEL), row(D_MODEL), row(nq)],
        out_shape=[jax.ShapeDtypeStruct((t, D_MODEL), F32), jax.ShapeDtypeStruct((t, D_MODEL), F32),
                   jax.ShapeDtypeStruct((t, nq), F32)],
        compiler_params=_cparams("parallel"),
    )(x, o_mla, h_f, h_b, l_g, o_swa, ow["g_a"], ow["g_b"], ow["g_c"], ow["w_a"], ow["w_b"], ow["w_c"], nffn, wq)


_INT_MAX = 2 ** 31 - 1


def _top_k_rows(s, k, tag=None):
    if tag is None:
        tag = lax.broadcasted_iota(I32, s.shape, 0)
    vals, picks = [], []
    for _ in range(k):
        m = jnp.max(s, axis=0, keepdims=True)
        idx = jnp.min(jnp.where(s == m, tag, _INT_MAX), axis=0, keepdims=True)
        vals.append(m)
        picks.append(idx)
        s = jnp.where(tag == idx, -jnp.inf, s)
    return jnp.concatenate(vals, axis=0), jnp.concatenate(picks, axis=0)


def _candidates(v1, v2):
    k, tm = v1.shape
    row8 = lax.broadcasted_iota(I32, (SUBLANES, tm), 0)
    row16 = lax.broadcasted_iota(I32, (k, tm), 0)
    sums = [v1[0:1] + v2, v1[1:2] + v2[0:SUBLANES], v1[SUBLANES:] + v2[0:1]]
    tags = [row16, k + row8, (row8 + SUBLANES) * k]
    mid = 3 * SUBLANES
    rowm = lax.broadcasted_iota(I32, (mid, tm), 0)
    arow = jnp.full((mid, tm), -1, I32)
    brow = jnp.zeros((mid, tm), I32)
    start = 0
    for a in range(2, SUBLANES):
        nb = k // (a + 1)
        inrun = (rowm >= start) & (rowm < start + nb)
        arow = jnp.where(inrun, a, arow)
        brow = jnp.where(inrun, rowm - start, brow)
        start += nb
    part_a = jnp.zeros((mid, tm), F32)
    part_b = jnp.zeros((mid, tm), F32)
    for a in range(2, SUBLANES):
        part_a = jnp.where(arow == a, v1[a:a + 1], part_a)
    for b in range(k // 3):
        part_b = jnp.where(brow == b, v2[b:b + 1], part_b)
    used = arow >= 0
    sums.append(jnp.where(used, part_a + part_b, -jnp.inf))
    tags.append(jnp.where(used, arow * k + brow, _INT_MAX))
    return jnp.concatenate(sums, axis=0), jnp.concatenate(tags, axis=0)


def _take_rows(table, idx):
    out = jnp.zeros(idx.shape, table.dtype)
    for a in range(table.shape[0]):
        out = jnp.where(idx == a, table[a:a + 1], out)
    return out


def _peer_topk_kernel(q_ref, keys_ref, id_out, gate_out):
    half = PEER_DKEY // 2
    dn = (((1,), (1,)), ((), ()))
    q = q_ref[...]
    s1 = lax.dot_general(keys_ref[0], q[:, :half], dn, preferred_element_type=F32, precision=lax.Precision.HIGHEST)
    s2 = lax.dot_general(keys_ref[1], q[:, half:], dn, preferred_element_type=F32, precision=lax.Precision.HIGHEST)
    v1, i1 = _top_k_rows(s1, PEER_TOPK)
    v2, i2 = _top_k_rows(s2, PEER_TOPK)
    cand, flat = _candidates(v1, v2)
    sc, pick = _top_k_rows(cand, PEER_TOPK, tag=flat)
    e = jnp.exp(sc - sc[0:1, :])
    rank1, rank2 = pick >> 4, pick & (PEER_TOPK - 1)
    id_out[0] = _take_rows(i1, rank1) * PEER_NKEYS + _take_rows(i2, rank2)
    gate_out[0] = e / jnp.sum(e, axis=0, keepdims=True)


def _peer_topk(q, subkeys, tm=512):
    t = q.shape[0]
    out = pl.BlockSpec((1, PEER_TOPK, tm), lambda i, h: (h, 0, i))
    return pl.pallas_call(
        _peer_topk_kernel, name="peer_topk",
        grid=(t // tm, PEER_HEADS),
        in_specs=[pl.BlockSpec((tm, PEER_DKEY), lambda i, h: (i, h)),
                  pl.BlockSpec(subkeys.shape, lambda i, h: (0, 0, 0))],
        out_specs=[out, out],
        out_shape=[jax.ShapeDtypeStruct((PEER_HEADS, PEER_TOPK, t), I32),
                   jax.ShapeDtypeStruct((PEER_HEADS, PEER_TOPK, t), F32)],
        compiler_params=_cparams("parallel", "parallel"),
    )(q, subkeys)


def _fold_halves(vals, sub):
    even = sub % 2 == 0
    first_two = sub % 4 < 2
    z0 = jnp.where(even, vals[0] + pltpu.roll(vals[0], 7, axis=0), vals[1] + pltpu.roll(vals[1], 1, axis=0))
    z1 = jnp.where(even, vals[2] + pltpu.roll(vals[2], 7, axis=0), vals[3] + pltpu.roll(vals[3], 1, axis=0))
    return jnp.where(first_two, z0 + pltpu.roll(z0, 6, axis=0), z1 + pltpu.roll(z1, 2, axis=0))


def _pack_table(w):
    bits = lax.bitcast_convert_type(w.astype(BF16), jnp.uint16).astype(jnp.uint32)
    half = D_MODEL // 2
    words = lax.bitcast_convert_type((bits[:, half:] << 16) | bits[:, :half], I32)
    return jnp.pad(words.reshape(-1, LANES), ((PACK_ROWS, PACK_ROWS), (0, 0)))


def _unpack(w):
    return pltpu.bitcast(w << 16, F32), pltpu.bitcast(w & -0x10000, F32)


def _peer_hidden_kernel(id_ref, x_ref, u_ref, h_out, r_scr):
    tm = x_ref.shape[0]
    sub = lax.broadcasted_iota(I32, (SUBLANES, LANES), 0)
    lower = sub < PACK_ROWS
    groups = PEER_SLOTS // SUBLANES
    ones = jnp.ones((SUBLANES, LANES), BF16)
    dn = (((1,), (1,)), ((), ()))

    def token(tt, g):
        t = g * SUBLANES + tt
        xt = x_ref[t]
        x_lo = jnp.where(lower, xt, pltpu.roll(xt, PACK_ROWS, axis=0))
        x_hi = jnp.where(lower, pltpu.roll(xt, PACK_ROWS, axis=0), xt)
        for j in range(groups):
            prods = []
            for i in range(PACK_ROWS):
                base = t * PEER_SLOTS + j * SUBLANES + i
                w_a = u_ref[pl.ds(pl.multiple_of(id_ref[base], PACK_ROWS), SUBLANES), :]
                w_b = u_ref[pl.ds(pl.multiple_of(id_ref[base + PACK_ROWS], PACK_ROWS), SUBLANES), :]
                lo, hi = _unpack(jnp.where(lower, w_a, w_b))
                prods.append(lo * x_lo + hi * x_hi)
            r_scr[pl.ds(pl.multiple_of(tt * PEER_SLOTS + j * SUBLANES, SUBLANES), SUBLANES), :] = _fold_halves(prods, sub)
        return g

    def group(g, _):
        lax.fori_loop(0, SUBLANES, token, g)
        r = r_scr[...]
        hi = r.astype(BF16)
        lo = (r - hi.astype(F32)).astype(BF16)
        sums = (lax.dot_general(ones, hi, dn, preferred_element_type=F32)
                + lax.dot_general(ones, lo, dn, preferred_element_type=F32))
        out = jnp.zeros((SUBLANES, LANES), F32)
        for tt in range(SUBLANES):
            out = jnp.where(sub == tt, sums[:, tt * PEER_SLOTS:(tt + 1) * PEER_SLOTS], out)
        h_out[pl.ds(pl.multiple_of(g * SUBLANES, SUBLANES), SUBLANES), :] = out
        return 0

    lax.fori_loop(0, tm // SUBLANES, group, 0)


def _smem_rows(tm):
    return pl.BlockSpec((tm * PEER_SLOTS,), lambda i: (i,), memory_space=pltpu.SMEM)


def _table_spec():
    return pl.BlockSpec((TABLE_ROWS, LANES), lambda i: (0, 0), pipeline_mode=pl.Buffered(1))


def _peer_hidden(rows, x3, u_packed, tm=256):
    t = x3.shape[0]
    return pl.pallas_call(
        _peer_hidden_kernel, name="peer_hidden",
        grid=(t // tm,),
        in_specs=[_smem_rows(tm), pl.BlockSpec((tm, SUBLANES, LANES), lambda i: (i, 0, 0)), _table_spec()],
        out_specs=pl.BlockSpec((tm, PEER_SLOTS), lambda i: (i, 0)),
        out_shape=jax.ShapeDtypeStruct((t, PEER_SLOTS), F32),
        scratch_shapes=[pltpu.VMEM((SUBLANES * PEER_SLOTS, LANES), F32)],
        compiler_params=_cparams("arbitrary"),
    )(rows, x3, u_packed)


def _peer_coef_kernel(h_ref, gate_ref, c_out):
    ct = (gate_ref[...] * _gelu(h_ref[...])).T
    for g in range(ct.shape[1] // SUBLANES):
        c_out[g] = ct if g == 0 else pltpu.roll(ct, LANES - SUBLANES * g, axis=1)


def _peer_coef(hid, gate, tm=LANES):
    t = gate.shape[0]
    return pl.pallas_call(
        _peer_coef_kernel, name="peer_coef",
        grid=(t // tm,),
        in_specs=[pl.BlockSpec((tm, PEER_SLOTS), lambda i: (i, 0)), pl.BlockSpec((tm, PEER_SLOTS), lambda i: (i, 0))],
        out_specs=pl.BlockSpec((tm // SUBLANES, PEER_SLOTS, LANES), lambda i: (i, 0, 0)),
        out_shape=jax.ShapeDtypeStruct((t // SUBLANES, PEER_SLOTS, LANES), F32),
        compiler_params=_cparams("parallel"),
    )(hid, gate)


def _peer_value_kernel(id_ref, c_ref, v_ref, y_out, cb_even, cb_odd):
    tm = y_out.shape[0]
    n_groups = tm // SUBLANES
    lower = lax.broadcasted_iota(I32, (SUBLANES, LANES), 0) < PACK_ROWS

    def spread(cb, g, e0, ne):
        rows = c_ref[g, pl.ds(e0, ne), :]
        for n in range(SUBLANES):
            cb[pl.ds(n * PEER_SLOTS + e0, ne), :] = jnp.broadcast_to(rows[:, n:n + 1], (ne, LANES))

    def run_group(g, cb_use, cb_fill):
        g_next = jnp.minimum(g + 1, n_groups - 1)
        per_token = PEER_SLOTS // SUBLANES

        def token(tt, _):
            t = g * SUBLANES + tt
            spread(cb_fill, g_next, pl.multiple_of(tt * per_token, per_token), per_token)
            cb_scr = cb_use

            acc_lo = jnp.zeros((SUBLANES, LANES), F32)
            acc_hi = jnp.zeros((SUBLANES, LANES), F32)
            for j in range(PEER_SLOTS // SUBLANES):
                t_lo, t_hi = [], []
                for s in range(SUBLANES):
                    e = j * SUBLANES + s
                    row = pl.multiple_of(id_ref[t * PEER_SLOTS + e], PACK_ROWS)
                    c = jnp.broadcast_to(cb_scr[pl.ds(tt * PEER_SLOTS + e, 1), :], (SUBLANES, LANES))
                    lo, hi = _unpack(v_ref[pl.ds(row, SUBLANES), :])
                    t_lo.append(c * lo)
                    t_hi.append(c * hi)
                while len(t_lo) > 1:
                    t_lo = [a + b for a, b in zip(t_lo[0::2], t_lo[1::2])]
                    t_hi = [a + b for a, b in zip(t_hi[0::2], t_hi[1::2])]
                acc_lo = acc_lo + t_lo[0]
                acc_hi = acc_hi + t_hi[0]
            y_out[t] = jnp.where(lower, acc_lo, pltpu.roll(acc_hi, PACK_ROWS, axis=0))
            return 0

        lax.fori_loop(0, SUBLANES, token, 0)

    def pair(k, _):
        run_group(2 * k, cb_even, cb_odd)
        run_group(2 * k + 1, cb_odd, cb_even)
        return 0

    spread(cb_even, 0, 0, PEER_SLOTS)
    lax.fori_loop(0, n_groups // 2, pair, 0)


def _peer_value(rows, coef, v_packed, tm=256):
    t = coef.shape[0] * SUBLANES
    return pl.pallas_call(
        _peer_value_kernel, name="peer_value",
        grid=(t // tm,),
        in_specs=[_smem_rows(tm), pl.BlockSpec((tm // SUBLANES, PEER_SLOTS, LANES), lambda i: (i, 0, 0)), _table_spec()],
        out_specs=pl.BlockSpec((tm, SUBLANES, LANES), lambda i: (i, 0, 0)),
        out_shape=jax.ShapeDtypeStruct((t, SUBLANES, LANES), F32),
        scratch_shapes=[pltpu.VMEM((SUBLANES * PEER_SLOTS, LANES), F32)] * 2,
        compiler_params=_cparams("arbitrary"),
    )(rows, coef, v_packed)


def _residual_kernel(x_ref, y_ref, g_ref, o_ref, *, final):
    x = x_ref[...] + y_ref[...]
    o_ref[...] = _rmsnorm(x, g_ref[...], D_MODEL) if final else x


def _residual(x1, y, gain, final, tm=512):
    t = x1.shape[0]
    return pl.pallas_call(
        functools.partial(_residual_kernel, final=final), name="peer_residual",
        grid=(t // tm,),
        in_specs=[pl.BlockSpec((tm, D_MODEL), lambda i: (i, 0)),
                  pl.BlockSpec((tm, D_MODEL), lambda i: (i, 0)),
                  pl.BlockSpec((1, D_MODEL), lambda i: (0, 0))],
        out_specs=pl.BlockSpec((tm, D_MODEL), lambda i: (i, 0)),
        out_shape=jax.ShapeDtypeStruct((t, D_MODEL), F32),
        compiler_params=_cparams("parallel"),
    )(x1, y, gain)


def _peer_ffn(x1, xn, q, subkeys, u, v, gain, final):
    t = x1.shape[0]
    ids_h, gate_h = _peer_topk(q, subkeys)
    ids = ids_h.reshape(PEER_SLOTS, t).T
    gate = gate_h.reshape(PEER_SLOTS, t).T
    rows = ids * PACK_ROWS + PACK_ROWS
    upper = (jnp.arange(PEER_SLOTS, dtype=I32) % SUBLANES) >= PACK_ROWS
    rows_u = jnp.where(upper[None, :], rows - PACK_ROWS, rows)
    hid = _peer_hidden(rows_u.reshape(-1), xn.reshape(t, SUBLANES, LANES), _pack_table(u))
    coef = _peer_coef(hid, gate)
    y = _peer_value(rows.reshape(-1), coef, _pack_table(v))
    return _residual(x1, y.reshape(t, D_MODEL), gain, final)


def kernel(x, positions, norm_mix, w_in, q_norm, w_uq, kv_norm, w_ukv, conv_w, conv_b, lru_wa, lru_ba, lru_wx, lru_bx,
           lru_lambda, swa_sink, grp_norm, w_out, norm_ffn, peer_wq, peer_subkeys, peer_u, peer_v, norm_final):
    batch, seq, _ = x.shape
    t = batch * seq
    depth = w_in.shape[0]
    xt = x.reshape(t, D_MODEL)

    pos = positions.astype(F32).reshape(t, 1)
    zeros = lambda n: jnp.zeros((n,), F32)
    inv_m, inv_s = _inv_freq(MLA_ROPE), _inv_freq(SWA_HEAD_DIM)
    inv = jnp.concatenate([zeros(MLA_NOPE), inv_m, inv_m, zeros(HP - MLA_NOPE - MLA_ROPE),
                           inv_s, inv_s, zeros(HP - SWA_HEAD_DIM)])
    cos, sin = _rope_tables(pos, inv[None, :])

    for l in range(depth):
        w_in_p, wq, wkv, w_gate, b_gate, ow = _layer_weights(
            l, w_in, w_uq, w_ukv, lru_wa, lru_ba, lru_wx, lru_bx, grp_norm, w_out)
        q, k, v, sq, sk, sv, lx, lg = _in_proj(
            xt, norm_mix[l][None, :], w_in_p, q_norm[l][None, :], wq, kv_norm[l][None, :], wkv, cos, sin)
        o_mla = _mla_attention(q, k, v, batch, seq)
        o_swa = _swa_attention(sq, sk, sv, swa_sink[l][None, :], batch, seq)
        h_f, h_b = _lru_scan(lx, conv_w[l], conv_b[l][None, :], w_gate, b_gate, lru_lambda[l], batch, seq)
        x1, xn, pq = _out_proj(xt, o_mla, h_f, h_b, lg, o_swa, ow, norm_ffn[l][None, :], peer_wq[l].astype(BF16))
        final = l == depth - 1
        gain = norm_final[None, :] if final else jnp.ones((1, D_MODEL), F32)
        xt = _peer_ffn(x1, xn, pq, peer_subkeys[l], peer_u[l], peer_v[l], gain, final)
    return xt.reshape(batch, seq, D_MODEL)
```

```python
import functools

import jax
import jax.numpy as jnp
from jax import lax
from jax.experimental import pallas as pl
from jax.experimental.pallas import tpu as pltpu

F32 = jnp.float32
BF16 = jnp.bfloat16
I32 = jnp.int32

D_MODEL = 1024
MLA_HEADS, MLA_NOPE, MLA_ROPE, MLA_V = 6, 64, 32, 64
MLA_Q_RANK, MLA_KV_RANK = 256, 128
LRU_WIDTH, LRU_C = 384, 8.0
SWA_Q_HEADS, SWA_KV_HEADS, SWA_HEAD_DIM, SWA_WINDOW = 4, 2, 64, 128
ROPE_THETA, EPS, NEG_BIG = 10000.0, 1e-6, -1e30
PEER_HEADS, PEER_NKEYS, PEER_DKEY, PEER_TOPK = 8, 128, 256, 16
PEER_SLOTS = PEER_HEADS * PEER_TOPK
PEER_EXPERTS = PEER_NKEYS * PEER_NKEYS

LANES = 128
SUBLANES = 8
HP = LANES
MLA_QK = MLA_HEADS * HP
SWA_Q = SWA_Q_HEADS * HP
SWA_KV = SWA_KV_HEADS * HP
VMEM_LIMIT = 56 * 1024 * 1024
PACK_ROWS = D_MODEL // (2 * LANES)
TABLE_ROWS = PEER_EXPERTS * PACK_ROWS + 2 * PACK_ROWS

_C_Q, _C_KV, _S_Q, _S_QR, _S_K, _S_KR, _S_V, _L_X, _L_G, _K_R, _IN_COLS = (
    0, 256, 384, 896, 1408, 1664, 1920, 2176, 2560, 2944, 3072)


def _cparams(*sem):
    return pltpu.CompilerParams(dimension_semantics=sem, vmem_limit_bytes=VMEM_LIMIT)


def _rmsnorm(x, g, n):
    return x * lax.rsqrt(jnp.sum(x * x, axis=-1, keepdims=True) * (1.0 / n) + EPS) * g


def _sigmoid(x):
    return 1.0 / (1.0 + jnp.exp(-x))


def _gelu(x):
    return 0.5 * x * (1.0 + lax.erf(x * (2.0 ** -0.5)))


def _rot_half_cols(w, d):
    k = w.shape[0]
    w3 = w.reshape(k, -1, d)
    return jnp.concatenate([-w3[..., d // 2:], w3[..., : d // 2]], axis=-1).reshape(k, -1)


def _pad_heads(w, d, lead=0):
    k = w.shape[0]
    w3 = w.reshape(k, -1, d)
    return jnp.pad(w3, ((0, 0), (0, 0), (lead, HP - d - lead))).reshape(k, -1)


def _inv_freq(d):
    return ROPE_THETA ** (-jnp.arange(0, d, 2, dtype=F32) / d)


def _layer_weights(l, w_in, w_uq, w_ukv, lru_wa, lru_ba, lru_wx, lru_bx, grp_norm, w_out):
    wi = w_in[l]
    c_q, c_kv, k_r = wi[:, 0:256], wi[:, 256:384], wi[:, 384:416]
    s_q, s_k, s_v = wi[:, 416:672], wi[:, 672:800], wi[:, 800:928]
    l_x, l_g = wi[:, 928:1312], wi[:, 1312:1696]
    k_r_blk = jnp.concatenate([jnp.zeros((D_MODEL, MLA_NOPE), F32), k_r, _rot_half_cols(k_r, MLA_ROPE)], axis=1)
    w_in_p = jnp.concatenate([
        c_q, c_kv,
        _pad_heads(s_q, SWA_HEAD_DIM), _pad_heads(_rot_half_cols(s_q, SWA_HEAD_DIM), SWA_HEAD_DIM),
        _pad_heads(s_k, SWA_HEAD_DIM), _pad_heads(_rot_half_cols(s_k, SWA_HEAD_DIM), SWA_HEAD_DIM),
        _pad_heads(s_v, SWA_HEAD_DIM), l_x, l_g, k_r_blk], axis=1).astype(BF16)

    wq3 = w_uq[l].reshape(MLA_Q_RANK, MLA_HEADS, MLA_NOPE + MLA_ROPE)
    q_nope = wq3[..., :MLA_NOPE].reshape(MLA_Q_RANK, -1)
    q_rope = wq3[..., MLA_NOPE:].reshape(MLA_Q_RANK, -1)
    wq = jnp.concatenate([
        _pad_heads(q_nope, MLA_NOPE) + _pad_heads(q_rope, MLA_ROPE, lead=MLA_NOPE),
        _pad_heads(_rot_half_cols(q_rope, MLA_ROPE), MLA_ROPE, lead=MLA_NOPE)], axis=1).astype(BF16)

    wkv3 = w_ukv[l].reshape(MLA_KV_RANK, MLA_HEADS, MLA_NOPE + MLA_V)
    wkv = jnp.concatenate([
        _pad_heads(wkv3[..., :MLA_NOPE].reshape(MLA_KV_RANK, -1), MLA_NOPE),
        _pad_heads(wkv3[..., MLA_NOPE:].reshape(MLA_KV_RANK, -1), MLA_V)], axis=1).astype(BF16)

    def blockdiag(w):
        n, bi, bj = w.shape
        eye = jnp.eye(n, dtype=w.dtype)
        return (w[:, :, None, :] * eye[:, None, :, None]).reshape(n * bi, n * bj)

    w_gate = [jnp.concatenate([blockdiag(lru_wa[l, d]), blockdiag(lru_wx[l, d])], axis=1).astype(BF16) for d in range(2)]
    b_gate = [jnp.concatenate([lru_ba[l, d].reshape(1, -1), lru_bx[l, d].reshape(1, -1)], axis=1) for d in range(2)]

    g = grp_norm[l]
    wo = w_out[l]
    ga, gb = MLA_HEADS * MLA_V, MLA_HEADS * MLA_V + LRU_WIDTH

    def pad_rows(w, d):
        return jnp.pad(w.reshape(-1, d, w.shape[-1]), ((0, 0), (0, HP - d), (0, 0))).reshape(-1, w.shape[-1])

    out_w = dict(
        g_a=_pad_heads(g[None, :ga], MLA_V), g_b=g[None, ga:gb], g_c=_pad_heads(g[None, gb:], SWA_HEAD_DIM),
        w_a=pad_rows(wo[:ga], MLA_V).astype(BF16), w_b=wo[ga:gb].astype(BF16),
        w_c=pad_rows(wo[gb:], SWA_HEAD_DIM).astype(BF16))
    return w_in_p, wq, wkv, w_gate, b_gate, out_w


def _rope_table_kernel(pos_ref, inv_ref, cos_ref, sin_ref):
    ang = pos_ref[...] * inv_ref[...]
    cos_ref[...] = jnp.cos(ang)
    sin_ref[...] = jnp.sin(ang)


def _rope_tables(pos, inv, tm=512):
    t, w = pos.shape[0], inv.shape[1]
    return pl.pallas_call(
        _rope_table_kernel, name="rope_tables",
        grid=(t // tm,),
        in_specs=[pl.BlockSpec((tm, 1), lambda i: (i, 0)), pl.BlockSpec((1, w), lambda i: (0, 0))],
        out_specs=[pl.BlockSpec((tm, w), lambda i: (i, 0))] * 2,
        out_shape=[jax.ShapeDtypeStruct((t, w), F32)] * 2,
        compiler_params=_cparams("parallel"),
    )(pos, inv)


def _in_proj_kernel(x_ref, nmix_ref, win_ref, qn_ref, wq_ref, kvn_ref, wkv_ref, cos_ref, sin_ref,
                    q_out, k_out, v_out, sq_out, sk_out, sv_out, lx_out, lg_out):
    xn = _rmsnorm(x_ref[...], nmix_ref[...], D_MODEL)
    proj = jnp.dot(xn.astype(BF16), win_ref[...], preferred_element_type=F32)
    cos, sin = cos_ref[...], sin_ref[...]
    cosq = jnp.concatenate([cos[:, :HP]] * MLA_HEADS, axis=1)
    sinq = jnp.concatenate([sin[:, :HP]] * MLA_HEADS, axis=1)
    coss = jnp.concatenate([cos[:, HP:]] * SWA_Q_HEADS, axis=1)
    sins = jnp.concatenate([sin[:, HP:]] * SWA_Q_HEADS, axis=1)

    swa_scale = SWA_HEAD_DIM ** -0.5
    sq_out[...] = ((proj[:, _S_Q:_S_QR] * coss + proj[:, _S_QR:_S_K] * sins) * swa_scale).astype(BF16)
    sk_out[...] = (proj[:, _S_K:_S_KR] * coss[:, :SWA_KV] + proj[:, _S_KR:_S_V] * sins[:, :SWA_KV]).astype(BF16)
    sv_out[...] = proj[:, _S_V:_L_X].astype(BF16)
    lx_out[...] = proj[:, _L_X:_L_G]
    lg_out[...] = proj[:, _L_G:_K_R]

    cqn = _rmsnorm(proj[:, _C_Q:_C_KV], qn_ref[...], MLA_Q_RANK).astype(BF16)
    qq = jnp.dot(cqn, wq_ref[...], preferred_element_type=F32)
    mla_scale = (MLA_NOPE + MLA_ROPE) ** -0.5
    q_out[...] = ((qq[:, :MLA_QK] * cosq + qq[:, MLA_QK:] * sinq) * mla_scale).astype(BF16)

    kvn = _rmsnorm(proj[:, _C_KV:_S_Q], kvn_ref[...], MLA_KV_RANK).astype(BF16)
    kv = jnp.dot(kvn, wkv_ref[...], preferred_element_type=F32)
    kr = proj[:, _K_R:_IN_COLS]
    lane = lax.broadcasted_iota(I32, kr.shape, 1)
    in_rope = (lane >= MLA_NOPE) & (lane < MLA_NOPE + MLA_ROPE)
    kr_rot = jnp.where(in_rope, kr * cosq[:, :HP] + pltpu.roll(kr, HP - MLA_ROPE, axis=1) * sinq[:, :HP], 0.0)
    for h in range(MLA_HEADS):
        k_out[:, h * HP:(h + 1) * HP] = (kv[:, h * HP:(h + 1) * HP] + kr_rot).astype(BF16)
    vlane = lax.broadcasted_iota(I32, (kv.shape[0], MLA_QK), 1)
    v_out[...] = jnp.where(vlane % HP == MLA_V, 1.0, kv[:, MLA_QK:]).astype(BF16)


def _in_proj(x, nmix, w_in_p, qn, wq, kvn, wkv, cos, sin, tm=256):
    t = x.shape[0]
    row = lambda w: pl.BlockSpec((tm, w), lambda i: (i, 0))
    full = lambda a: pl.BlockSpec(a.shape, lambda i: (0,) * a.ndim)
    widths = (MLA_QK, MLA_QK, MLA_QK, SWA_Q, SWA_KV, SWA_KV, LRU_WIDTH, LRU_WIDTH)
    dtypes = (BF16, BF16, BF16, BF16, BF16, BF16, F32, F32)
    return pl.pallas_call(
        _in_proj_kernel, name="in_proj",
        grid=(t // tm,),
        in_specs=[row(D_MODEL), full(nmix), full(w_in_p), full(qn), full(wq), full(kvn), full(wkv),
                  row(2 * HP), row(2 * HP)],
        out_specs=[row(w) for w in widths],
        out_shape=[jax.ShapeDtypeStruct((t, w), d) for w, d in zip(widths, dtypes)],
        compiler_params=_cparams("parallel"),
    )(x, nmix, w_in_p, qn, wq, kvn, wkv, cos, sin)


def _mla_kernel(q_ref, k_ref, v_ref, o_ref, *, tk):
    q = q_ref[...]
    tq = q.shape[0]
    nk = k_ref.shape[0] // tk

    def body(j, carry):
        m, acc = carry
        off = pl.multiple_of(j * tk, tk)
        k = k_ref[pl.ds(off, tk), :]
        v = v_ref[pl.ds(off, tk), :]
        s = lax.dot_general(q, k, (((1,), (1,)), ((), ())), preferred_element_type=F32)
        m_new = jnp.maximum(m, jnp.max(s, axis=-1, keepdims=True))
        p = jnp.exp(s - m_new)
        acc = jnp.exp(m - m_new) * acc + jnp.dot(p.astype(BF16), v, preferred_element_type=F32)
        return m_new, acc

    init = (jnp.full((tq, 1), -jnp.inf, F32), jnp.zeros((tq, HP), F32))
    _, acc = lax.fori_loop(0, nk, body, init)
    lane = lax.broadcasted_iota(I32, acc.shape, 1)
    o_ref[...] = jnp.where(lane < MLA_V, acc / acc[:, MLA_V:MLA_V + 1], 0.0)


def _mla_attention(q, k, v, batch, seq, tq=2048, tk=1024):
    tq, tk = min(tq, seq), min(tk, seq)
    nq = seq // tq
    return pl.pallas_call(
        functools.partial(_mla_kernel, tk=tk), name="mla_attention",
        grid=(batch, MLA_HEADS, nq),
        in_specs=[pl.BlockSpec((tq, HP), lambda b, h, i: (b * nq + i, h)),
                  pl.BlockSpec((seq, HP), lambda b, h, i: (b, h)),
                  pl.BlockSpec((seq, HP), lambda b, h, i: (b, h))],
        out_specs=pl.BlockSpec((tq, HP), lambda b, h, i: (b * nq + i, h)),
        out_shape=jax.ShapeDtypeStruct((batch * seq, MLA_QK), F32),
        compiler_params=_cparams("parallel", "parallel", "arbitrary"),
    )(q, k, v)


def _swa_kernel(sink_ref, q_ref, kp_ref, kc_ref, kn_ref, vp_ref, vc_ref, vn_ref, o_ref, *, seq):
    w = SWA_WINDOW
    tq = q_ref.shape[0]
    i = pl.program_id(1)
    kcat = jnp.concatenate([kp_ref[...], kc_ref[...], kn_ref[...]], axis=0)
    vcat = jnp.concatenate([vp_ref[...], vc_ref[...], vn_ref[...]], axis=0)
    for j in range(tq // w):
        qpos = i * tq + j * w + lax.broadcasted_iota(I32, (w, 3 * w), 0)
        kpos = i * tq + (j - 1) * w + lax.broadcasted_iota(I32, (w, 3 * w), 1)
        valid = (jnp.abs(kpos - qpos) <= w) & (kpos >= 0) & (kpos < seq)
        for h in range(SWA_Q_HEADS):
            kh = h // (SWA_Q_HEADS // SWA_KV_HEADS)
            qh = q_ref[j * w:(j + 1) * w, h * HP:(h + 1) * HP]
            kj = kcat[j * w:(j + 3) * w, kh * HP:(kh + 1) * HP]
            vj = vcat[j * w:(j + 3) * w, kh * HP:(kh + 1) * HP]
            s = lax.dot_general(qh, kj, (((1,), (1,)), ((), ())), preferred_element_type=F32)
            s = jnp.where(valid, s, NEG_BIG)
            sink = sink_ref[0, h]
            m = jnp.maximum(jnp.max(s, axis=-1, keepdims=True), sink)
            p = jnp.exp(s - m)
            den = jnp.sum(p, axis=-1, keepdims=True) + jnp.exp(sink - m)
            o_ref[j * w:(j + 1) * w, h * HP:(h + 1) * HP] = jnp.dot(p.astype(BF16), vj, preferred_element_type=F32) / den


def _swa_attention(sq, sk, sv, sink, batch, seq, tq=512):
    w = SWA_WINDOW
    nq, nw, r = seq // tq, seq // w, tq // w
    prev = pl.BlockSpec((w, SWA_KV), lambda b, i: (b * nw + jnp.maximum(i * r - 1, 0), 0))
    cur = pl.BlockSpec((tq, SWA_KV), lambda b, i: (b * nq + i, 0))
    nxt = pl.BlockSpec((w, SWA_KV), lambda b, i: (b * nw + jnp.minimum((i + 1) * r, nw - 1), 0))
    return pl.pallas_call(
        functools.partial(_swa_kernel, seq=seq), name="swa_attention",
        grid=(batch, nq),
        in_specs=[pl.BlockSpec(memory_space=pltpu.SMEM),
                  pl.BlockSpec((tq, SWA_Q), lambda b, i: (b * nq + i, 0)),
                  prev, cur, nxt, prev, cur, nxt],
        out_specs=pl.BlockSpec((tq, SWA_Q), lambda b, i: (b * nq + i, 0)),
        out_shape=jax.ShapeDtypeStruct((batch * seq, SWA_Q), F32),
        compiler_params=_cparams("parallel", "parallel"),
    )(sink, sq, sk, sk, sk, sv, sv, sv)


def _lru_kernel(xpf_ref, xcf_ref, xnf_ref, xpb_ref, xcb_ref, xnb_ref, cw_ref, cb_ref, wf_ref, bf_ref, wb_ref, bb_ref,
                lam_ref, hf_out, hb_out, carry_f, carry_b, *, nt):
    tm = xcf_ref.shape[0]
    i = pl.program_id(1)
    halo = xpf_ref.shape[0]

    @pl.when(i == 0)
    def _():
        carry_f[...] = jnp.zeros_like(carry_f)
        carry_b[...] = jnp.zeros_like(carry_b)

    def gates(xp_ref, xc_ref, xn_ref, first, last, w_ref, b_ref, lam):
        prev = jnp.where(first, 0.0, xp_ref[...])
        nxt = jnp.where(last, 0.0, xn_ref[...])
        xcat = jnp.concatenate([prev, xc_ref[...], nxt], axis=0)
        cw = cw_ref[...]
        conv = cb_ref[...]
        for tap in range(cw.shape[0]):
            conv = conv + cw[tap:tap + 1, :] * xcat[halo - 1 + tap: halo - 1 + tap + tm, :]
        g = jnp.dot(conv.astype(BF16), w_ref[...], preferred_element_type=F32) + b_ref[...]
        r = _sigmoid(g[:, :LRU_WIDTH])
        gate_i = _sigmoid(g[:, LRU_WIDTH:])
        softplus = jnp.maximum(-lam, 0.0) + jnp.log1p(jnp.exp(-jnp.abs(lam)))
        log_a = -LRU_C * r * softplus
        a = jnp.exp(log_a)
        b = jnp.sqrt(1.0 - a * a) * (gate_i * conv)
        return a, b

    row = lax.broadcasted_iota(I32, (tm, LRU_WIDTH), 0)

    def scan(a, b, reverse):
        k = 1
        while k < tm:
            if reverse:
                keep = row < tm - k
                shift = tm - k
            else:
                keep = row >= k
                shift = k
            a_s = jnp.where(keep, pltpu.roll(a, shift, axis=0), 1.0)
            b_s = jnp.where(keep, pltpu.roll(b, shift, axis=0), 0.0)
            b = a * b_s + b
            a = a * a_s
            k *= 2
        return a, b

    a, b = gates(xpf_ref, xcf_ref, xnf_ref, i == 0, i == nt - 1, wf_ref, bf_ref, lam_ref[0:1, :])
    a, b = scan(a, b, False)
    h = a * carry_f[...] + b
    hf_out[...] = h
    carry_f[...] = h[tm - 1:tm, :]

    a, b = gates(xpb_ref, xcb_ref, xnb_ref, i == nt - 1, i == 0, wb_ref, bb_ref, lam_ref[1:2, :])
    a, b = scan(a, b, True)
    h = a * carry_b[...] + b
    hb_out[...] = h
    carry_b[...] = h[0:1, :]


def _lru_scan(lx, conv_w, conv_b, w_gate, b_gate, lam, batch, seq, tm=256):
    nt, hb = seq // tm, seq // SUBLANES
    r = tm // SUBLANES
    fwd = lambda b, i: i
    bwd = lambda b, i: nt - 1 - i

    def specs(tile):
        return [pl.BlockSpec((SUBLANES, LRU_WIDTH), lambda b, i: (b * hb + jnp.maximum(tile(b, i) * r - 1, 0), 0)),
                pl.BlockSpec((tm, LRU_WIDTH), lambda b, i: (b * nt + tile(b, i), 0)),
                pl.BlockSpec((SUBLANES, LRU_WIDTH), lambda b, i: (b * hb + jnp.minimum((tile(b, i) + 1) * r, hb - 1), 0))]

    full = lambda a: pl.BlockSpec(a.shape, lambda b, i: (0,) * a.ndim)
    return pl.pallas_call(
        functools.partial(_lru_kernel, nt=nt), name="lru_scan",
        grid=(batch, nt),
        in_specs=specs(fwd) + specs(bwd) + [full(conv_w), full(conv_b), full(w_gate[0]), full(b_gate[0]),
                                            full(w_gate[1]), full(b_gate[1]), full(lam)],
        out_specs=[pl.BlockSpec((tm, LRU_WIDTH), lambda b, i: (b * nt + i, 0)),
                   pl.BlockSpec((tm, LRU_WIDTH), lambda b, i: (b * nt + nt - 1 - i, 0))],
        out_shape=[jax.ShapeDtypeStruct((batch * seq, LRU_WIDTH), F32)] * 2,
        scratch_shapes=[pltpu.VMEM((1, LRU_WIDTH), F32), pltpu.VMEM((1, LRU_WIDTH), F32)],
        compiler_params=_cparams("parallel", "arbitrary"),
    )(lx, lx, lx, lx, lx, lx, conv_w, conv_b, w_gate[0], b_gate[0], w_gate[1], b_gate[1], lam)


def _out_proj_kernel(x_ref, oa_ref, hf_ref, hb_ref, lg_ref, oc_ref, ga_ref, gb_ref, gc_ref, wa_ref, wb_ref, wc_ref,
                     nffn_ref, wq_ref, x1_out, xn_out, q_out):
    mix_a = _rmsnorm(oa_ref[...], ga_ref[...], MLA_HEADS * MLA_V).astype(BF16)
    o_lru = (hf_ref[...] + hb_ref[...]) * _gelu(lg_ref[...])
    mix_b = _rmsnorm(o_lru, gb_ref[...], LRU_WIDTH).astype(BF16)
    mix_c = _rmsnorm(oc_ref[...], gc_ref[...], SWA_Q_HEADS * SWA_HEAD_DIM).astype(BF16)
    x1 = (x_ref[...]
          + jnp.dot(mix_a, wa_ref[...], preferred_element_type=F32)
          + jnp.dot(mix_b, wb_ref[...], preferred_element_type=F32)
          + jnp.dot(mix_c, wc_ref[...], preferred_element_type=F32))
    x1_out[...] = x1
    xn = _rmsnorm(x1, nffn_ref[...], D_MODEL)
    xn_out[...] = xn
    q_out[...] = jnp.dot(xn.astype(BF16), wq_ref[...], preferred_element_type=F32)


def _out_proj(x, o_mla, h_f, h_b, l_g, o_swa, ow, nffn, wq, tm=256):
    t = x.shape[0]
    row = lambda w: pl.BlockSpec((tm, w), lambda i: (i, 0))
    full = lambda a: pl.BlockSpec(a.shape, lambda i: (0,) * a.ndim)
    nq = wq.shape[1]
    return pl.pallas_call(
        _out_proj_kernel, name="out_proj",
        grid=(t // tm,),
        in_specs=[row(D_MODEL), row(MLA_QK), row(LRU_WIDTH), row(LRU_WIDTH), row(LRU_WIDTH), row(SWA_Q),
                  full(ow["g_a"]), full(ow["g_b"]), full(ow["g_c"]), full(ow["w_a"]), full(ow["w_b"]), full(ow["w_c"]),
                  full(nffn), full(wq)],
        out_specs=[row(D_MODEL), row(D_MODEL), row(nq)],
        out_shape=[jax.ShapeDtypeStruct((t, D_MODEL), F32), jax.ShapeDtypeStruct((t, D_MODEL), F32),
                   jax.ShapeDtypeStruct((t, nq), F32)],
        compiler_params=_cparams("parallel"),
    )(x, o_mla, h_f, h_b, l_g, o_swa, ow["g_a"], ow["g_b"], ow["g_c"], ow["w_a"], ow["w_b"], ow["w_c"], nffn, wq)


_INT_MAX = 2 ** 31 - 1


def _top_k_rows(s, k, tag=None):
    if tag is None:
        tag = lax.broadcasted_iota(I32, s.shape, 0)
    vals, picks = [], []
    for _ in range(k):
        m = jnp.max(s, axis=0, keepdims=True)
        idx = jnp.min(jnp.where(s == m, tag, _INT_MAX), axis=0, keepdims=True)
        vals.append(m)
        picks.append(idx)
        s = jnp.where(tag == idx, -jnp.inf, s)
    return jnp.concatenate(vals, axis=0), jnp.concatenate(picks, axis=0)


def _candidates(v1, v2):
    k, tm = v1.shape
    row8 = lax.broadcasted_iota(I32, (SUBLANES, tm), 0)
    row16 = lax.broadcasted_iota(I32, (k, tm), 0)
    sums = [v1[0:1] + v2, v1[1:2] + v2[0:SUBLANES], v1[SUBLANES:] + v2[0:1]]
    tags = [row16, k + row8, (row8 + SUBLANES) * k]
    mid = 3 * SUBLANES
    rowm = lax.broadcasted_iota(I32, (mid, tm), 0)
    arow = jnp.full((mid, tm), -1, I32)
    brow = jnp.zeros((mid, tm), I32)
    start = 0
    for a in range(2, SUBLANES):
        nb = k // (a + 1)
        inrun = (rowm >= start) & (rowm < start + nb)
        arow = jnp.where(inrun, a, arow)
        brow = jnp.where(inrun, rowm - start, brow)
        start += nb
    part_a = jnp.zeros((mid, tm), F32)
    part_b = jnp.zeros((mid, tm), F32)
    for a in range(2, SUBLANES):
        part_a = jnp.where(arow == a, v1[a:a + 1], part_a)
    for b in range(k // 3):
        part_b = jnp.where(brow == b, v2[b:b + 1], part_b)
    used = arow >= 0
    sums.append(jnp.where(used, part_a + part_b, -jnp.inf))
    tags.append(jnp.where(used, arow * k + brow, _INT_MAX))
    return jnp.concatenate(sums, axis=0), jnp.concatenate(tags, axis=0)


def _take_rows(table, idx):
    out = jnp.zeros(idx.shape, table.dtype)
    for a in range(table.shape[0]):
        out = jnp.where(idx == a, table[a:a + 1], out)
    return out


def _peer_topk_kernel(q_ref, keys_ref, id_out, gate_out):
    half = PEER_DKEY // 2
    dn = (((1,), (1,)), ((), ()))
    q = q_ref[...]
    s1 = lax.dot_general(keys_ref[0], q[:, :half], dn, preferred_element_type=F32, precision=lax.Precision.HIGHEST)
    s2 = lax.dot_general(keys_ref[1], q[:, half:], dn, preferred_element_type=F32, precision=lax.Precision.HIGHEST)
    v1, i1 = _top_k_rows(s1, PEER_TOPK)
    v2, i2 = _top_k_rows(s2, PEER_TOPK)
    cand, flat = _candidates(v1, v2)
    sc, pick = _top_k_rows(cand, PEER_TOPK, tag=flat)
    e = jnp.exp(sc - sc[0:1, :])
    rank1, rank2 = pick >> 4, pick & (PEER_TOPK - 1)
    id_out[0] = _take_rows(i1, rank1) * PEER_NKEYS + _take_rows(i2, rank2)
    gate_out[0] = e / jnp.sum(e, axis=0, keepdims=True)


def _peer_topk(q, subkeys, tm=512):
    t = q.shape[0]
    out = pl.BlockSpec((1, PEER_TOPK, tm), lambda i, h: (h, 0, i))
    return pl.pallas_call(
        _peer_topk_kernel, name="peer_topk",
        grid=(t // tm, PEER_HEADS),
        in_specs=[pl.BlockSpec((tm, PEER_DKEY), lambda i, h: (i, h)),
                  pl.BlockSpec(subkeys.shape, lambda i, h: (0, 0, 0))],
        out_specs=[out, out],
        out_shape=[jax.ShapeDtypeStruct((PEER_HEADS, PEER_TOPK, t), I32),
                   jax.ShapeDtypeStruct((PEER_HEADS, PEER_TOPK, t), F32)],
        compiler_params=_cparams("parallel", "parallel"),
    )(q, subkeys)


def _fold_halves(vals, sub):
    even = sub % 2 == 0
    first_two = sub % 4 < 2
    z0 = jnp.where(even, vals[0] + pltpu.roll(vals[0], 7, axis=0), vals[1] + pltpu.roll(vals[1], 1, axis=0))
    z1 = jnp.where(even, vals[2] + pltpu.roll(vals[2], 7, axis=0), vals[3] + pltpu.roll(vals[3], 1, axis=0))
    return jnp.where(first_two, z0 + pltpu.roll(z0, 6, axis=0), z1 + pltpu.roll(z1, 2, axis=0))


def _pack_table(w):
    bits = lax.bitcast_convert_type(w.astype(BF16), jnp.uint16).astype(jnp.uint32)
    half = D_MODEL // 2
    words = lax.bitcast_convert_type((bits[:, half:] << 16) | bits[:, :half], I32)
    return jnp.pad(words.reshape(-1, LANES), ((PACK_ROWS, PACK_ROWS), (0, 0)))


def _unpack(w):
    return pltpu.bitcast(w << 16, F32), pltpu.bitcast(w & -0x10000, F32)


def _peer_hidden_kernel(id_ref, x_ref, u_ref, h_out, r_scr):
    tm = x_ref.shape[0]
    sub = lax.broadcasted_iota(I32, (SUBLANES, LANES), 0)
    lower = sub < PACK_ROWS
    groups = PEER_SLOTS // SUBLANES
    ones = jnp.ones((SUBLANES, 2 * LANES), BF16)
    dn = (((1,), (1,)), ((), ()))

    def token(tt, g):
        t = g * SUBLANES + tt
        xt = x_ref[t]
        x_lo = jnp.where(lower, xt, pltpu.roll(xt, PACK_ROWS, axis=0))
        x_hi = jnp.where(lower, pltpu.roll(xt, PACK_ROWS, axis=0), xt)
        for j in range(groups):
            prods = []
            for i in range(PACK_ROWS):
                base = t * PEER_SLOTS + j * SUBLANES + i
                w_a = u_ref[pl.ds(pl.multiple_of(id_ref[base], PACK_ROWS), SUBLANES), :]
                w_b = u_ref[pl.ds(pl.multiple_of(id_ref[base + PACK_ROWS], PACK_ROWS), SUBLANES), :]
                lo, hi = _unpack(jnp.where(lower, w_a, w_b))
                prods.append(lo * x_lo + hi * x_hi)
            r_scr[pl.ds(pl.multiple_of(tt * PEER_SLOTS + j * SUBLANES, SUBLANES), SUBLANES), :] = _fold_halves(prods, sub)
        return g

    def group(g, _):
        lax.fori_loop(0, SUBLANES, token, g)
        r = r_scr[...]
        hi = r.astype(BF16)
        lo = (r - hi.astype(F32)).astype(BF16)
        sums = lax.dot_general(ones, jnp.concatenate([hi, lo], axis=1), dn, preferred_element_type=F32)
        out = jnp.zeros((SUBLANES, LANES), F32)
        for tt in range(SUBLANES):
            out = jnp.where(sub == tt, sums[:, tt * PEER_SLOTS:(tt + 1) * PEER_SLOTS], out)
        h_out[pl.ds(pl.multiple_of(g * SUBLANES, SUBLANES), SUBLANES), :] = out
        return 0

    lax.fori_loop(0, tm // SUBLANES, group, 0)


def _smem_rows(tm):
    return pl.BlockSpec((tm * PEER_SLOTS,), lambda i: (i,), memory_space=pltpu.SMEM)


def _table_spec():
    return pl.BlockSpec((TABLE_ROWS, LANES), lambda i: (0, 0), pipeline_mode=pl.Buffered(1))


def _peer_hidden(rows, x3, u_packed, tm=256):
    t = x3.shape[0]
    return pl.pallas_call(
        _peer_hidden_kernel, name="peer_hidden",
        grid=(t // tm,),
        in_specs=[_smem_rows(tm), pl.BlockSpec((tm, SUBLANES, LANES), lambda i: (i, 0, 0)), _table_spec()],
        out_specs=pl.BlockSpec((tm, PEER_SLOTS), lambda i: (i, 0)),
        out_shape=jax.ShapeDtypeStruct((t, PEER_SLOTS), F32),
        scratch_shapes=[pltpu.VMEM((SUBLANES * PEER_SLOTS, LANES), F32)],
        compiler_params=_cparams("arbitrary"),
    )(rows, x3, u_packed)


def _peer_coef_kernel(h_ref, gate_ref, c_out):
    ct = (gate_ref[...] * _gelu(h_ref[...])).T
    for g in range(ct.shape[1] // SUBLANES):
        c_out[g] = ct if g == 0 else pltpu.roll(ct, LANES - SUBLANES * g, axis=1)


def _peer_coef(hid, gate, tm=LANES):
    t = gate.shape[0]
    return pl.pallas_call(
        _peer_coef_kernel, name="peer_coef",
        grid=(t // tm,),
        in_specs=[pl.BlockSpec((tm, PEER_SLOTS), lambda i: (i, 0)), pl.BlockSpec((tm, PEER_SLOTS), lambda i: (i, 0))],
        out_specs=pl.BlockSpec((tm // SUBLANES, PEER_SLOTS, LANES), lambda i: (i, 0, 0)),
        out_shape=jax.ShapeDtypeStruct((t // SUBLANES, PEER_SLOTS, LANES), F32),
        compiler_params=_cparams("parallel"),
    )(hid, gate)


def _peer_value_kernel(id_ref, c_ref, x_ref, v_ref, y_out, cb_even, cb_odd):
    tm = y_out.shape[0]
    n_groups = tm // SUBLANES
    lower = lax.broadcasted_iota(I32, (SUBLANES, LANES), 0) < PACK_ROWS

    def spread(cb, g, e0, ne):
        rows = c_ref[g, pl.ds(e0, ne), :]
        for n in range(SUBLANES):
            cb[pl.ds(n * PEER_SLOTS + e0, ne), :] = jnp.broadcast_to(rows[:, n:n + 1], (ne, LANES))

    def run_group(g, cb_use, cb_fill):
        g_next = jnp.minimum(g + 1, n_groups - 1)
        per_token = PEER_SLOTS // SUBLANES

        def token(tt, _):
            t = g * SUBLANES + tt
            spread(cb_fill, g_next, pl.multiple_of(tt * per_token, per_token), per_token)
            cb_scr = cb_use

            acc_lo = jnp.zeros((SUBLANES, LANES), F32)
            acc_hi = jnp.zeros((SUBLANES, LANES), F32)
            for j in range(PEER_SLOTS // SUBLANES):
                t_lo, t_hi = [], []
                for s in range(SUBLANES):
                    e = j * SUBLANES + s
                    row = pl.multiple_of(id_ref[t * PEER_SLOTS + e], PACK_ROWS)
                    c = jnp.broadcast_to(cb_scr[pl.ds(tt * PEER_SLOTS + e, 1), :], (SUBLANES, LANES))
                    lo, hi = _unpack(v_ref[pl.ds(row, SUBLANES), :])
                    t_lo.append(c * lo)
                    t_hi.append(c * hi)
                while len(t_lo) > 1:
                    t_lo = [a + b for a, b in zip(t_lo[0::2], t_lo[1::2])]
                    t_hi = [a + b for a, b in zip(t_hi[0::2], t_hi[1::2])]
                acc_lo = acc_lo + t_lo[0]
                acc_hi = acc_hi + t_hi[0]
            y_out[t] = x_ref[t] + jnp.where(lower, acc_lo, pltpu.roll(acc_hi, PACK_ROWS, axis=0))
            return 0

        lax.fori_loop(0, SUBLANES, token, 0)

    def pair(k, _):
        run_group(2 * k, cb_even, cb_odd)
        run_group(2 * k + 1, cb_odd, cb_even)
        return 0

    spread(cb_even, 0, 0, PEER_SLOTS)
    lax.fori_loop(0, n_groups // 2, pair, 0)


def _peer_value(rows, coef, x3, v_packed, tm=256):
    t = x3.shape[0]
    tile = pl.BlockSpec((tm, SUBLANES, LANES), lambda i: (i, 0, 0))
    return pl.pallas_call(
        _peer_value_kernel, name="peer_value",
        grid=(t // tm,),
        in_specs=[_smem_rows(tm), pl.BlockSpec((tm // SUBLANES, PEER_SLOTS, LANES), lambda i: (i, 0, 0)), tile,
                  _table_spec()],
        out_specs=tile,
        out_shape=jax.ShapeDtypeStruct((t, SUBLANES, LANES), F32),
        scratch_shapes=[pltpu.VMEM((SUBLANES * PEER_SLOTS, LANES), F32)] * 2,
        compiler_params=_cparams("arbitrary"),
    )(rows, coef, x3, v_packed)


def _final_norm_kernel(x_ref, g_ref, o_ref):
    o_ref[...] = _rmsnorm(x_ref[...], g_ref[...], D_MODEL)


def _final_norm(x, gain, tm=512):
    t = x.shape[0]
    return pl.pallas_call(
        _final_norm_kernel, name="final_norm",
        grid=(t // tm,),
        in_specs=[pl.BlockSpec((tm, D_MODEL), lambda i: (i, 0)), pl.BlockSpec((1, D_MODEL), lambda i: (0, 0))],
        out_specs=pl.BlockSpec((tm, D_MODEL), lambda i: (i, 0)),
        out_shape=jax.ShapeDtypeStruct((t, D_MODEL), F32),
        compiler_params=_cparams("parallel"),
    )(x, gain)


def _peer_ffn(x1, xn, q, subkeys, u, v):
    t = x1.shape[0]
    ids_h, gate_h = _peer_topk(q, subkeys)
    ids = ids_h.reshape(PEER_SLOTS, t).T
    gate = gate_h.reshape(PEER_SLOTS, t).T
    rows = ids * PACK_ROWS + PACK_ROWS
    upper = (jnp.arange(PEER_SLOTS, dtype=I32) % SUBLANES) >= PACK_ROWS
    rows_u = jnp.where(upper[None, :], rows - PACK_ROWS, rows)
    hid = _peer_hidden(rows_u.reshape(-1), xn.reshape(t, SUBLANES, LANES), _pack_table(u))
    coef = _peer_coef(hid, gate)
    out = _peer_value(rows.reshape(-1), coef, x1.reshape(t, SUBLANES, LANES), _pack_table(v))
    return out.reshape(t, D_MODEL)


def kernel(x, positions, norm_mix, w_in, q_norm, w_uq, kv_norm, w_ukv, conv_w, conv_b, lru_wa, lru_ba, lru_wx, lru_bx,
           lru_lambda, swa_sink, grp_norm, w_out, norm_ffn, peer_wq, peer_subkeys, peer_u, peer_v, norm_final):
    batch, seq, _ = x.shape
    t = batch * seq
    depth = w_in.shape[0]
    xt = x.reshape(t, D_MODEL)

    pos = positions.astype(F32).reshape(t, 1)
    zeros = lambda n: jnp.zeros((n,), F32)
    inv_m, inv_s = _inv_freq(MLA_ROPE), _inv_freq(SWA_HEAD_DIM)
    inv = jnp.concatenate([zeros(MLA_NOPE), inv_m, inv_m, zeros(HP - MLA_NOPE - MLA_ROPE),
                           inv_s, inv_s, zeros(HP - SWA_HEAD_DIM)])
    cos, sin = _rope_tables(pos, inv[None, :])

    for l in range(depth):
        w_in_p, wq, wkv, w_gate, b_gate, ow = _layer_weights(
            l, w_in, w_uq, w_ukv, lru_wa, lru_ba, lru_wx, lru_bx, grp_norm, w_out)
        q, k, v, sq, sk, sv, lx, lg = _in_proj(
            xt, norm_mix[l][None, :], w_in_p, q_norm[l][None, :], wq, kv_norm[l][None, :], wkv, cos, sin)
        o_mla = _mla_attention(q, k, v, batch, seq)
        o_swa = _swa_attention(sq, sk, sv, swa_sink[l][None, :], batch, seq)
        h_f, h_b = _lru_scan(lx, conv_w[l], conv_b[l][None, :], w_gate, b_gate, lru_lambda[l], batch, seq)
        x1, xn, pq = _out_proj(xt, o_mla, h_f, h_b, lg, o_swa, ow, norm_ffn[l][None, :], peer_wq[l].astype(BF16))
        xt = _peer_ffn(x1, xn, pq, peer_subkeys[l], peer_u[l], peer_v[l])
    return _final_norm(xt, norm_final[None, :]).reshape(batch, seq, D_MODEL)
```

```python
import functools

import jax
import jax.numpy as jnp
from jax import lax
from jax.experimental import pallas as pl
from jax.experimental.pallas import tpu as pltpu

F32 = jnp.float32
BF16 = jnp.bfloat16
I32 = jnp.int32

D_MODEL = 1024
MLA_HEADS, MLA_NOPE, MLA_ROPE, MLA_V = 6, 64, 32, 64
MLA_Q_RANK, MLA_KV_RANK = 256, 128
LRU_WIDTH, LRU_C = 384, 8.0
SWA_Q_HEADS, SWA_KV_HEADS, SWA_HEAD_DIM, SWA_WINDOW = 4, 2, 64, 128
ROPE_THETA, EPS, NEG_BIG = 10000.0, 1e-6, -1e30
PEER_HEADS, PEER_NKEYS, PEER_DKEY, PEER_TOPK = 8, 128, 256, 16
PEER_SLOTS = PEER_HEADS * PEER_TOPK
PEER_EXPERTS = PEER_NKEYS * PEER_NKEYS

LANES = 128
SUBLANES = 8
HP = LANES
MLA_QK = MLA_HEADS * HP
SWA_Q = SWA_Q_HEADS * HP
SWA_KV = SWA_KV_HEADS * HP
VMEM_LIMIT = 56 * 1024 * 1024
PACK_ROWS = D_MODEL // (2 * LANES)
TABLE_ROWS = PEER_EXPERTS * PACK_ROWS + 2 * PACK_ROWS

_C_Q, _C_KV, _S_Q, _S_QR, _S_K, _S_KR, _S_V, _L_X, _L_G, _K_R, _IN_COLS = (
    0, 256, 384, 896, 1408, 1664, 1920, 2176, 2560, 2944, 3072)


def _cparams(*sem):
    return pltpu.CompilerParams(dimension_semantics=sem, vmem_limit_bytes=VMEM_LIMIT)


def _rmsnorm(x, g, n):
    return x * lax.rsqrt(jnp.sum(x * x, axis=-1, keepdims=True) * (1.0 / n) + EPS) * g


def _sigmoid(x):
    return 1.0 / (1.0 + jnp.exp(-x))


def _gelu(x):
    return 0.5 * x * (1.0 + lax.erf(x * (2.0 ** -0.5)))


def _rot_half_cols(w, d):
    k = w.shape[0]
    w3 = w.reshape(k, -1, d)
    return jnp.concatenate([-w3[..., d // 2:], w3[..., : d // 2]], axis=-1).reshape(k, -1)


def _pad_heads(w, d, lead=0):
    k = w.shape[0]
    w3 = w.reshape(k, -1, d)
    return jnp.pad(w3, ((0, 0), (0, 0), (lead, HP - d - lead))).reshape(k, -1)


def _inv_freq(d):
    return ROPE_THETA ** (-jnp.arange(0, d, 2, dtype=F32) / d)


def _layer_weights(l, w_in, w_uq, w_ukv, lru_wa, lru_ba, lru_wx, lru_bx, grp_norm, w_out):
    wi = w_in[l]
    c_q, c_kv, k_r = wi[:, 0:256], wi[:, 256:384], wi[:, 384:416]
    s_q, s_k, s_v = wi[:, 416:672], wi[:, 672:800], wi[:, 800:928]
    l_x, l_g = wi[:, 928:1312], wi[:, 1312:1696]
    k_r_blk = jnp.concatenate([jnp.zeros((D_MODEL, MLA_NOPE), F32), k_r, _rot_half_cols(k_r, MLA_ROPE)], axis=1)
    w_in_p = jnp.concatenate([
        c_q, c_kv,
        _pad_heads(s_q, SWA_HEAD_DIM), _pad_heads(_rot_half_cols(s_q, SWA_HEAD_DIM), SWA_HEAD_DIM),
        _pad_heads(s_k, SWA_HEAD_DIM), _pad_heads(_rot_half_cols(s_k, SWA_HEAD_DIM), SWA_HEAD_DIM),
        _pad_heads(s_v, SWA_HEAD_DIM), l_x, l_g, k_r_blk], axis=1).astype(BF16)

    wq3 = w_uq[l].reshape(MLA_Q_RANK, MLA_HEADS, MLA_NOPE + MLA_ROPE)
    q_nope = wq3[..., :MLA_NOPE].reshape(MLA_Q_RANK, -1)
    q_rope = wq3[..., MLA_NOPE:].reshape(MLA_Q_RANK, -1)
    wq = jnp.concatenate([
        _pad_heads(q_nope, MLA_NOPE) + _pad_heads(q_rope, MLA_ROPE, lead=MLA_NOPE),
        _pad_heads(_rot_half_cols(q_rope, MLA_ROPE), MLA_ROPE, lead=MLA_NOPE)], axis=1).astype(BF16)

    wkv3 = w_ukv[l].reshape(MLA_KV_RANK, MLA_HEADS, MLA_NOPE + MLA_V)
    wkv = jnp.concatenate([
        _pad_heads(wkv3[..., :MLA_NOPE].reshape(MLA_KV_RANK, -1), MLA_NOPE),
        _pad_heads(wkv3[..., MLA_NOPE:].reshape(MLA_KV_RANK, -1), MLA_V)], axis=1).astype(BF16)

    def blockdiag(w):
        n, bi, bj = w.shape
        eye = jnp.eye(n, dtype=w.dtype)
        return (w[:, :, None, :] * eye[:, None, :, None]).reshape(n * bi, n * bj)

    w_gate = [jnp.concatenate([blockdiag(lru_wa[l, d]), blockdiag(lru_wx[l, d])], axis=1).astype(BF16) for d in range(2)]
    b_gate = [jnp.concatenate([lru_ba[l, d].reshape(1, -1), lru_bx[l, d].reshape(1, -1)], axis=1) for d in range(2)]

    g = grp_norm[l]
    wo = w_out[l]
    ga, gb = MLA_HEADS * MLA_V, MLA_HEADS * MLA_V + LRU_WIDTH

    def pad_rows(w, d):
        return jnp.pad(w.reshape(-1, d, w.shape[-1]), ((0, 0), (0, HP - d), (0, 0))).reshape(-1, w.shape[-1])

    out_w = dict(
        g_a=_pad_heads(g[None, :ga], MLA_V), g_b=g[None, ga:gb], g_c=_pad_heads(g[None, gb:], SWA_HEAD_DIM),
        w_a=pad_rows(wo[:ga], MLA_V).astype(BF16), w_b=wo[ga:gb].astype(BF16),
        w_c=pad_rows(wo[gb:], SWA_HEAD_DIM).astype(BF16))
    return w_in_p, wq, wkv, w_gate, b_gate, out_w


def _rope_table_kernel(pos_ref, inv_ref, cos_ref, sin_ref):
    ang = pos_ref[...] * inv_ref[...]
    cos_ref[...] = jnp.cos(ang)
    sin_ref[...] = jnp.sin(ang)


def _rope_tables(pos, inv, tm=512):
    t, w = pos.shape[0], inv.shape[1]
    return pl.pallas_call(
        _rope_table_kernel, name="rope_tables",
        grid=(t // tm,),
        in_specs=[pl.BlockSpec((tm, 1), lambda i: (i, 0)), pl.BlockSpec((1, w), lambda i: (0, 0))],
        out_specs=[pl.BlockSpec((tm, w), lambda i: (i, 0))] * 2,
        out_shape=[jax.ShapeDtypeStruct((t, w), F32)] * 2,
        compiler_params=_cparams("parallel"),
    )(pos, inv)


def _in_proj_kernel(x_ref, nmix_ref, win_ref, qn_ref, wq_ref, kvn_ref, wkv_ref, cos_ref, sin_ref,
                    q_out, k_out, v_out, sq_out, sk_out, sv_out, lx_out, lg_out):
    xn = _rmsnorm(x_ref[...], nmix_ref[...], D_MODEL)
    proj = jnp.dot(xn.astype(BF16), win_ref[...], preferred_element_type=F32)
    cos, sin = cos_ref[...], sin_ref[...]
    cosq = jnp.concatenate([cos[:, :HP]] * MLA_HEADS, axis=1)
    sinq = jnp.concatenate([sin[:, :HP]] * MLA_HEADS, axis=1)
    coss = jnp.concatenate([cos[:, HP:]] * SWA_Q_HEADS, axis=1)
    sins = jnp.concatenate([sin[:, HP:]] * SWA_Q_HEADS, axis=1)

    swa_scale = SWA_HEAD_DIM ** -0.5
    sq_out[...] = ((proj[:, _S_Q:_S_QR] * coss + proj[:, _S_QR:_S_K] * sins) * swa_scale).astype(BF16)
    sk_out[...] = (proj[:, _S_K:_S_KR] * coss[:, :SWA_KV] + proj[:, _S_KR:_S_V] * sins[:, :SWA_KV]).astype(BF16)
    sv_out[...] = proj[:, _S_V:_L_X].astype(BF16)
    lx_out[...] = proj[:, _L_X:_L_G]
    lg_out[...] = proj[:, _L_G:_K_R]

    cqn = _rmsnorm(proj[:, _C_Q:_C_KV], qn_ref[...], MLA_Q_RANK).astype(BF16)
    qq = jnp.dot(cqn, wq_ref[...], preferred_element_type=F32)
    mla_scale = (MLA_NOPE + MLA_ROPE) ** -0.5
    q_out[...] = ((qq[:, :MLA_QK] * cosq + qq[:, MLA_QK:] * sinq) * mla_scale).astype(BF16)

    kvn = _rmsnorm(proj[:, _C_KV:_S_Q], kvn_ref[...], MLA_KV_RANK).astype(BF16)
    kv = jnp.dot(kvn, wkv_ref[...], preferred_element_type=F32)
    kr = proj[:, _K_R:_IN_COLS]
    lane = lax.broadcasted_iota(I32, kr.shape, 1)
    in_rope = (lane >= MLA_NOPE) & (lane < MLA_NOPE + MLA_ROPE)
    kr_rot = jnp.where(in_rope, kr * cosq[:, :HP] + pltpu.roll(kr, HP - MLA_ROPE, axis=1) * sinq[:, :HP], 0.0)
    for h in range(MLA_HEADS):
        k_out[:, h * HP:(h + 1) * HP] = (kv[:, h * HP:(h + 1) * HP] + kr_rot).astype(BF16)
    vlane = lax.broadcasted_iota(I32, (kv.shape[0], MLA_QK), 1)
    v_out[...] = jnp.where(vlane % HP == MLA_V, 1.0, kv[:, MLA_QK:]).astype(BF16)


def _in_proj(x, nmix, w_in_p, qn, wq, kvn, wkv, cos, sin, tm=256):
    t = x.shape[0]
    row = lambda w: pl.BlockSpec((tm, w), lambda i: (i, 0))
    full = lambda a: pl.BlockSpec(a.shape, lambda i: (0,) * a.ndim)
    widths = (MLA_QK, MLA_QK, MLA_QK, SWA_Q, SWA_KV, SWA_KV, LRU_WIDTH, LRU_WIDTH)
    dtypes = (BF16, BF16, BF16, BF16, BF16, BF16, F32, F32)
    return pl.pallas_call(
        _in_proj_kernel, name="in_proj",
        grid=(t // tm,),
        in_specs=[row(D_MODEL), full(nmix), full(w_in_p), full(qn), full(wq), full(kvn), full(wkv),
                  row(2 * HP), row(2 * HP)],
        out_specs=[row(w) for w in widths],
        out_shape=[jax.ShapeDtypeStruct((t, w), d) for w, d in zip(widths, dtypes)],
        compiler_params=_cparams("parallel"),
    )(x, nmix, w_in_p, qn, wq, kvn, wkv, cos, sin)


def _mla_kernel(q_ref, k_ref, v_ref, o_ref, *, tk):
    q = q_ref[...]
    tq = q.shape[0]
    nk = k_ref.shape[0] // tk

    def body(j, carry):
        m, acc = carry
        off = pl.multiple_of(j * tk, tk)
        k = k_ref[pl.ds(off, tk), :]
        v = v_ref[pl.ds(off, tk), :]
        s = lax.dot_general(q, k, (((1,), (1,)), ((), ())), preferred_element_type=F32)
        m_new = jnp.maximum(m, jnp.max(s, axis=-1, keepdims=True))
        p = jnp.exp(s - m_new)
        acc = jnp.exp(m - m_new) * acc + jnp.dot(p.astype(BF16), v, preferred_element_type=F32)
        return m_new, acc

    init = (jnp.full((tq, 1), -jnp.inf, F32), jnp.zeros((tq, HP), F32))
    _, acc = lax.fori_loop(0, nk, body, init)
    lane = lax.broadcasted_iota(I32, acc.shape, 1)
    o_ref[...] = jnp.where(lane < MLA_V, acc / acc[:, MLA_V:MLA_V + 1], 0.0)


def _mla_attention(q, k, v, batch, seq, tq=4096, tk=1024):
    tq, tk = min(tq, seq), min(tk, seq)
    nq = seq // tq
    return pl.pallas_call(
        functools.partial(_mla_kernel, tk=tk), name="mla_attention",
        grid=(batch, MLA_HEADS, nq),
        in_specs=[pl.BlockSpec((tq, HP), lambda b, h, i: (b * nq + i, h)),
                  pl.BlockSpec((seq, HP), lambda b, h, i: (b, h)),
                  pl.BlockSpec((seq, HP), lambda b, h, i: (b, h))],
        out_specs=pl.BlockSpec((tq, HP), lambda b, h, i: (b * nq + i, h)),
        out_shape=jax.ShapeDtypeStruct((batch * seq, MLA_QK), F32),
        compiler_params=_cparams("parallel", "parallel", "arbitrary"),
    )(q, k, v)


def _swa_kernel(sink_ref, q_ref, kp_ref, kc_ref, kn_ref, vp_ref, vc_ref, vn_ref, o_ref, *, seq):
    w = SWA_WINDOW
    tq = q_ref.shape[0]
    i = pl.program_id(1)
    kcat = jnp.concatenate([kp_ref[...], kc_ref[...], kn_ref[...]], axis=0)
    vcat = jnp.concatenate([vp_ref[...], vc_ref[...], vn_ref[...]], axis=0)
    for j in range(tq // w):
        qpos = i * tq + j * w + lax.broadcasted_iota(I32, (w, 3 * w), 0)
        kpos = i * tq + (j - 1) * w + lax.broadcasted_iota(I32, (w, 3 * w), 1)
        valid = (jnp.abs(kpos - qpos) <= w) & (kpos >= 0) & (kpos < seq)
        for h in range(SWA_Q_HEADS):
            kh = h // (SWA_Q_HEADS // SWA_KV_HEADS)
            qh = q_ref[j * w:(j + 1) * w, h * HP:(h + 1) * HP]
            kj = kcat[j * w:(j + 3) * w, kh * HP:(kh + 1) * HP]
            vj = vcat[j * w:(j + 3) * w, kh * HP:(kh + 1) * HP]
            s = lax.dot_general(qh, kj, (((1,), (1,)), ((), ())), preferred_element_type=F32)
            s = jnp.where(valid, s, NEG_BIG)
            sink = sink_ref[0, h]
            m = jnp.maximum(jnp.max(s, axis=-1, keepdims=True), sink)
            p = jnp.exp(s - m)
            den = jnp.sum(p, axis=-1, keepdims=True) + jnp.exp(sink - m)
            o_ref[j * w:(j + 1) * w, h * HP:(h + 1) * HP] = jnp.dot(p.astype(BF16), vj, preferred_element_type=F32) / den


def _swa_attention(sq, sk, sv, sink, batch, seq, tq=512):
    w = SWA_WINDOW
    nq, nw, r = seq // tq, seq // w, tq // w
    prev = pl.BlockSpec((w, SWA_KV), lambda b, i: (b * nw + jnp.maximum(i * r - 1, 0), 0))
    cur = pl.BlockSpec((tq, SWA_KV), lambda b, i: (b * nq + i, 0))
    nxt = pl.BlockSpec((w, SWA_KV), lambda b, i: (b * nw + jnp.minimum((i + 1) * r, nw - 1), 0))
    return pl.pallas_call(
        functools.partial(_swa_kernel, seq=seq), name="swa_attention",
        grid=(batch, nq),
        in_specs=[pl.BlockSpec(memory_space=pltpu.SMEM),
                  pl.BlockSpec((tq, SWA_Q), lambda b, i: (b * nq + i, 0)),
                  prev, cur, nxt, prev, cur, nxt],
        out_specs=pl.BlockSpec((tq, SWA_Q), lambda b, i: (b * nq + i, 0)),
        out_shape=jax.ShapeDtypeStruct((batch * seq, SWA_Q), F32),
        compiler_params=_cparams("parallel", "parallel"),
    )(sink, sq, sk, sk, sk, sv, sv, sv)


def _lru_kernel(xpf_ref, xcf_ref, xnf_ref, xpb_ref, xcb_ref, xnb_ref, cw_ref, cb_ref, wf_ref, bf_ref, wb_ref, bb_ref,
                lam_ref, hf_out, hb_out, carry_f, carry_b, *, nt):
    tm = xcf_ref.shape[0]
    i = pl.program_id(1)
    halo = xpf_ref.shape[0]

    @pl.when(i == 0)
    def _():
        carry_f[...] = jnp.zeros_like(carry_f)
        carry_b[...] = jnp.zeros_like(carry_b)

    def gates(xp_ref, xc_ref, xn_ref, first, last, w_ref, b_ref, lam):
        prev = jnp.where(first, 0.0, xp_ref[...])
        nxt = jnp.where(last, 0.0, xn_ref[...])
        xcat = jnp.concatenate([prev, xc_ref[...], nxt], axis=0)
        cw = cw_ref[...]
        conv = cb_ref[...]
        for tap in range(cw.shape[0]):
            conv = conv + cw[tap:tap + 1, :] * xcat[halo - 1 + tap: halo - 1 + tap + tm, :]
        g = jnp.dot(conv.astype(BF16), w_ref[...], preferred_element_type=F32) + b_ref[...]
        r = _sigmoid(g[:, :LRU_WIDTH])
        gate_i = _sigmoid(g[:, LRU_WIDTH:])
        softplus = jnp.maximum(-lam, 0.0) + jnp.log1p(jnp.exp(-jnp.abs(lam)))
        log_a = -LRU_C * r * softplus
        a = jnp.exp(log_a)
        b = jnp.sqrt(1.0 - a * a) * (gate_i * conv)
        return a, b

    row = lax.broadcasted_iota(I32, (tm, LRU_WIDTH), 0)

    def scan(a, b, reverse):
        k = 1
        while k < tm:
            if reverse:
                keep = row < tm - k
                shift = tm - k
            else:
                keep = row >= k
                shift = k
            a_s = jnp.where(keep, pltpu.roll(a, shift, axis=0), 1.0)
            b_s = jnp.where(keep, pltpu.roll(b, shift, axis=0), 0.0)
            b = a * b_s + b
            a = a * a_s
            k *= 2
        return a, b

    a, b = gates(xpf_ref, xcf_ref, xnf_ref, i == 0, i == nt - 1, wf_ref, bf_ref, lam_ref[0:1, :])
    a, b = scan(a, b, False)
    h = a * carry_f[...] + b
    hf_out[...] = h
    carry_f[...] = h[tm - 1:tm, :]

    a, b = gates(xpb_ref, xcb_ref, xnb_ref, i == nt - 1, i == 0, wb_ref, bb_ref, lam_ref[1:2, :])
    a, b = scan(a, b, True)
    h = a * carry_b[...] + b
    hb_out[...] = h
    carry_b[...] = h[0:1, :]


def _lru_scan(lx, conv_w, conv_b, w_gate, b_gate, lam, batch, seq, tm=256):
    nt, hb = seq // tm, seq // SUBLANES
    r = tm // SUBLANES
    fwd = lambda b, i: i
    bwd = lambda b, i: nt - 1 - i

    def specs(tile):
        return [pl.BlockSpec((SUBLANES, LRU_WIDTH), lambda b, i: (b * hb + jnp.maximum(tile(b, i) * r - 1, 0), 0)),
                pl.BlockSpec((tm, LRU_WIDTH), lambda b, i: (b * nt + tile(b, i), 0)),
                pl.BlockSpec((SUBLANES, LRU_WIDTH), lambda b, i: (b * hb + jnp.minimum((tile(b, i) + 1) * r, hb - 1), 0))]

    full = lambda a: pl.BlockSpec(a.shape, lambda b, i: (0,) * a.ndim)
    return pl.pallas_call(
        functools.partial(_lru_kernel, nt=nt), name="lru_scan",
        grid=(batch, nt),
        in_specs=specs(fwd) + specs(bwd) + [full(conv_w), full(conv_b), full(w_gate[0]), full(b_gate[0]),
                                            full(w_gate[1]), full(b_gate[1]), full(lam)],
        out_specs=[pl.BlockSpec((tm, LRU_WIDTH), lambda b, i: (b * nt + i, 0)),
                   pl.BlockSpec((tm, LRU_WIDTH), lambda b, i: (b * nt + nt - 1 - i, 0))],
        out_shape=[jax.ShapeDtypeStruct((batch * seq, LRU_WIDTH), F32)] * 2,
        scratch_shapes=[pltpu.VMEM((1, LRU_WIDTH), F32), pltpu.VMEM((1, LRU_WIDTH), F32)],
        compiler_params=_cparams("parallel", "arbitrary"),
    )(lx, lx, lx, lx, lx, lx, conv_w, conv_b, w_gate[0], b_gate[0], w_gate[1], b_gate[1], lam)


def _out_proj_kernel(x_ref, oa_ref, hf_ref, hb_ref, lg_ref, oc_ref, ga_ref, gb_ref, gc_ref, wa_ref, wb_ref, wc_ref,
                     nffn_ref, wq_ref, x1_out, xn_out, q_out):
    mix_a = _rmsnorm(oa_ref[...], ga_ref[...], MLA_HEADS * MLA_V).astype(BF16)
    o_lru = (hf_ref[...] + hb_ref[...]) * _gelu(lg_ref[...])
    mix_b = _rmsnorm(o_lru, gb_ref[...], LRU_WIDTH).astype(BF16)
    mix_c = _rmsnorm(oc_ref[...], gc_ref[...], SWA_Q_HEADS * SWA_HEAD_DIM).astype(BF16)
    x1 = (x_ref[...]
          + jnp.dot(mix_a, wa_ref[...], preferred_element_type=F32)
          + jnp.dot(mix_b, wb_ref[...], preferred_element_type=F32)
          + jnp.dot(mix_c, wc_ref[...], preferred_element_type=F32))
    x1_out[...] = x1
    xn = _rmsnorm(x1, nffn_ref[...], D_MODEL)
    xn_out[...] = xn
    q_out[...] = jnp.dot(xn.astype(BF16), wq_ref[...], preferred_element_type=F32)


def _out_proj(x, o_mla, h_f, h_b, l_g, o_swa, ow, nffn, wq, tm=256):
    t = x.shape[0]
    row = lambda w: pl.BlockSpec((tm, w), lambda i: (i, 0))
    full = lambda a: pl.BlockSpec(a.shape, lambda i: (0,) * a.ndim)
    nq = wq.shape[1]
    return pl.pallas_call(
        _out_proj_kernel, name="out_proj",
        grid=(t // tm,),
        in_specs=[row(D_MODEL), row(MLA_QK), row(LRU_WIDTH), row(LRU_WIDTH), row(LRU_WIDTH), row(SWA_Q),
                  full(ow["g_a"]), full(ow["g_b"]), full(ow["g_c"]), full(ow["w_a"]), full(ow["w_b"]), full(ow["w_c"]),
                  full(nffn), full(wq)],
        out_specs=[row(D_MODEL), row(D_MODEL), row(nq)],
        out_shape=[jax.ShapeDtypeStruct((t, D_MODEL), F32), jax.ShapeDtypeStruct((t, D_MODEL), F32),
                   jax.ShapeDtypeStruct((t, nq), F32)],
        compiler_params=_cparams("parallel"),
    )(x, o_mla, h_f, h_b, l_g, o_swa, ow["g_a"], ow["g_b"], ow["g_c"], ow["w_a"], ow["w_b"], ow["w_c"], nffn, wq)


_INT_MAX = 2 ** 31 - 1


def _top_k_rows(s, k, tag=None):
    if tag is None:
        tag = lax.broadcasted_iota(I32, s.shape, 0)
    vals, picks = [], []
    for _ in range(k):
        m = jnp.max(s, axis=0, keepdims=True)
        idx = jnp.min(jnp.where(s == m, tag, _INT_MAX), axis=0, keepdims=True)
        vals.append(m)
        picks.append(idx)
        s = jnp.where(tag == idx, -jnp.inf, s)
    return jnp.concatenate(vals, axis=0), jnp.concatenate(picks, axis=0)


def _candidates(v1, v2):
    k, tm = v1.shape
    row8 = lax.broadcasted_iota(I32, (SUBLANES, tm), 0)
    row16 = lax.broadcasted_iota(I32, (k, tm), 0)
    sums = [v1[0:1] + v2, v1[1:2] + v2[0:SUBLANES], v1[SUBLANES:] + v2[0:1]]
    tags = [row16, k + row8, (row8 + SUBLANES) * k]
    mid = 3 * SUBLANES
    rowm = lax.broadcasted_iota(I32, (mid, tm), 0)
    arow = jnp.full((mid, tm), -1, I32)
    brow = jnp.zeros((mid, tm), I32)
    start = 0
    for a in range(2, SUBLANES):
        nb = k // (a + 1)
        inrun = (rowm >= start) & (rowm < start + nb)
        arow = jnp.where(inrun, a, arow)
        brow = jnp.where(inrun, rowm - start, brow)
        start += nb
    part_a = jnp.zeros((mid, tm), F32)
    part_b = jnp.zeros((mid, tm), F32)
    for a in range(2, SUBLANES):
        part_a = jnp.where(arow == a, v1[a:a + 1], part_a)
    for b in range(k // 3):
        part_b = jnp.where(brow == b, v2[b:b + 1], part_b)
    used = arow >= 0
    sums.append(jnp.where(used, part_a + part_b, -jnp.inf))
    tags.append(jnp.where(used, arow * k + brow, _INT_MAX))
    return jnp.concatenate(sums, axis=0), jnp.concatenate(tags, axis=0)


def _take_rows(table, idx):
    out = jnp.zeros(idx.shape, table.dtype)
    for a in range(table.shape[0]):
        out = jnp.where(idx == a, table[a:a + 1], out)
    return out


def _peer_topk_kernel(q_ref, keys_ref, id_out, gate_out):
    half = PEER_DKEY // 2
    dn = (((1,), (1,)), ((), ()))
    q = q_ref[...]
    s1 = lax.dot_general(keys_ref[0], q[:, :half], dn, preferred_element_type=F32, precision=lax.Precision.HIGHEST)
    s2 = lax.dot_general(keys_ref[1], q[:, half:], dn, preferred_element_type=F32, precision=lax.Precision.HIGHEST)
    v1, i1 = _top_k_rows(s1, PEER_TOPK)
    v2, i2 = _top_k_rows(s2, PEER_TOPK)
    cand, flat = _candidates(v1, v2)
    sc, pick = _top_k_rows(cand, PEER_TOPK, tag=flat)
    e = jnp.exp(sc - sc[0:1, :])
    rank1, rank2 = pick >> 4, pick & (PEER_TOPK - 1)
    id_out[0] = _take_rows(i1, rank1) * PEER_NKEYS + _take_rows(i2, rank2)
    gate_out[0] = e / jnp.sum(e, axis=0, keepdims=True)


def _peer_topk(q, subkeys, tm=1024):
    t = q.shape[0]
    out = pl.BlockSpec((1, PEER_TOPK, tm), lambda i, h: (h, 0, i))
    return pl.pallas_call(
        _peer_topk_kernel, name="peer_topk",
        grid=(t // tm, PEER_HEADS),
        in_specs=[pl.BlockSpec((tm, PEER_DKEY), lambda i, h: (i, h)),
                  pl.BlockSpec(subkeys.shape, lambda i, h: (0, 0, 0))],
        out_specs=[out, out],
        out_shape=[jax.ShapeDtypeStruct((PEER_HEADS, PEER_TOPK, t), I32),
                   jax.ShapeDtypeStruct((PEER_HEADS, PEER_TOPK, t), F32)],
        compiler_params=_cparams("parallel", "parallel"),
    )(q, subkeys)


def _fold_halves(vals, sub):
    even = sub % 2 == 0
    first_two = sub % 4 < 2
    z0 = jnp.where(even, vals[0] + pltpu.roll(vals[0], 7, axis=0), vals[1] + pltpu.roll(vals[1], 1, axis=0))
    z1 = jnp.where(even, vals[2] + pltpu.roll(vals[2], 7, axis=0), vals[3] + pltpu.roll(vals[3], 1, axis=0))
    return jnp.where(first_two, z0 + pltpu.roll(z0, 6, axis=0), z1 + pltpu.roll(z1, 2, axis=0))


def _pack_table(w):
    bits = lax.bitcast_convert_type(w.astype(BF16), jnp.uint16).astype(jnp.uint32)
    half = D_MODEL // 2
    words = lax.bitcast_convert_type((bits[:, half:] << 16) | bits[:, :half], I32)
    return jnp.pad(words.reshape(-1, LANES), ((PACK_ROWS, PACK_ROWS), (0, 0)))


def _unpack(w):
    return pltpu.bitcast(w << 16, F32), pltpu.bitcast(w & -0x10000, F32)


def _peer_hidden_kernel(id_ref, x_ref, u_ref, h_out, r_scr):
    tm = x_ref.shape[0]
    sub = lax.broadcasted_iota(I32, (SUBLANES, LANES), 0)
    lower = sub < PACK_ROWS
    groups = PEER_SLOTS // SUBLANES
    ones = jnp.ones((SUBLANES, 2 * LANES), BF16)
    dn = (((1,), (1,)), ((), ()))

    def token(tt, g):
        t = g * SUBLANES + tt
        xt = x_ref[t]
        x_lo = jnp.where(lower, xt, pltpu.roll(xt, PACK_ROWS, axis=0))
        x_hi = jnp.where(lower, pltpu.roll(xt, PACK_ROWS, axis=0), xt)
        for j in range(groups):
            prods = []
            for i in range(PACK_ROWS):
                base = t * PEER_SLOTS + j * SUBLANES + i
                w_a = u_ref[pl.ds(pl.multiple_of(id_ref[base], PACK_ROWS), SUBLANES), :]
                w_b = u_ref[pl.ds(pl.multiple_of(id_ref[base + PACK_ROWS], PACK_ROWS), SUBLANES), :]
                lo, hi = _unpack(jnp.where(lower, w_a, w_b))
                prods.append(lo * x_lo + hi * x_hi)
            r_scr[pl.ds(pl.multiple_of(tt * PEER_SLOTS + j * SUBLANES, SUBLANES), SUBLANES), :] = _fold_halves(prods, sub)
        return g

    def group(g, _):
        lax.fori_loop(0, SUBLANES, token, g)
        r = r_scr[...]
        hi = r.astype(BF16)
        lo = (r - hi.astype(F32)).astype(BF16)
        sums = lax.dot_general(ones, jnp.concatenate([hi, lo], axis=1), dn, preferred_element_type=F32)
        out = jnp.zeros((SUBLANES, LANES), F32)
        for tt in range(SUBLANES):
            out = jnp.where(sub == tt, sums[:, tt * PEER_SLOTS:(tt + 1) * PEER_SLOTS], out)
        h_out[pl.ds(pl.multiple_of(g * SUBLANES, SUBLANES), SUBLANES), :] = out
        return 0

    lax.fori_loop(0, tm // SUBLANES, group, 0)


def _smem_rows(tm):
    return pl.BlockSpec((tm * PEER_SLOTS,), lambda i: (i,), memory_space=pltpu.SMEM)


def _table_spec():
    return pl.BlockSpec((TABLE_ROWS, LANES), lambda i: (0, 0), pipeline_mode=pl.Buffered(1))


def _peer_hidden(rows, x3, u_packed, tm=256):
    t = x3.shape[0]
    return pl.pallas_call(
        _peer_hidden_kernel, name="peer_hidden",
        grid=(t // tm,),
        in_specs=[_smem_rows(tm), pl.BlockSpec((tm, SUBLANES, LANES), lambda i: (i, 0, 0)), _table_spec()],
        out_specs=pl.BlockSpec((tm, PEER_SLOTS), lambda i: (i, 0)),
        out_shape=jax.ShapeDtypeStruct((t, PEER_SLOTS), F32),
        scratch_shapes=[pltpu.VMEM((SUBLANES * PEER_SLOTS, LANES), F32)],
        compiler_params=_cparams("arbitrary"),
    )(rows, x3, u_packed)


def _peer_coef_kernel(h_ref, gate_ref, c_out):
    ct = (gate_ref[...] * _gelu(h_ref[...])).T
    for g in range(ct.shape[1] // SUBLANES):
        c_out[g] = ct if g == 0 else pltpu.roll(ct, LANES - SUBLANES * g, axis=1)


def _peer_coef(hid, gate, tm=LANES):
    t = gate.shape[0]
    return pl.pallas_call(
        _peer_coef_kernel, name="peer_coef",
        grid=(t // tm,),
        in_specs=[pl.BlockSpec((tm, PEER_SLOTS), lambda i: (i, 0)), pl.BlockSpec((tm, PEER_SLOTS), lambda i: (i, 0))],
        out_specs=pl.BlockSpec((tm // SUBLANES, PEER_SLOTS, LANES), lambda i: (i, 0, 0)),
        out_shape=jax.ShapeDtypeStruct((t // SUBLANES, PEER_SLOTS, LANES), F32),
        compiler_params=_cparams("parallel"),
    )(hid, gate)


def _peer_value_kernel(id_ref, c_ref, x_ref, v_ref, y_out, cb_even, cb_odd):
    tm = y_out.shape[0]
    n_groups = tm // SUBLANES
    lower = lax.broadcasted_iota(I32, (SUBLANES, LANES), 0) < PACK_ROWS

    def spread(cb, g, e0, ne):
        rows = c_ref[g, pl.ds(e0, ne), :]
        for n in range(SUBLANES):
            cb[pl.ds(n * PEER_SLOTS + e0, ne), :] = jnp.broadcast_to(rows[:, n:n + 1], (ne, LANES))

    def run_group(g, cb_use, cb_fill):
        g_next = jnp.minimum(g + 1, n_groups - 1)
        per_token = PEER_SLOTS // SUBLANES

        def token(tt, _):
            t = g * SUBLANES + tt
            spread(cb_fill, g_next, pl.multiple_of(tt * per_token, per_token), per_token)
            cb_scr = cb_use

            acc_lo = jnp.zeros((SUBLANES, LANES), F32)
            acc_hi = jnp.zeros((SUBLANES, LANES), F32)
            for j in range(PEER_SLOTS // SUBLANES):
                t_lo, t_hi = [], []
                for s in range(SUBLANES):
                    e = j * SUBLANES + s
                    row = pl.multiple_of(id_ref[t * PEER_SLOTS + e], PACK_ROWS)
                    c = jnp.broadcast_to(cb_scr[pl.ds(tt * PEER_SLOTS + e, 1), :], (SUBLANES, LANES))
                    lo, hi = _unpack(v_ref[pl.ds(row, SUBLANES), :])
                    t_lo.append(c * lo)
                    t_hi.append(c * hi)
                while len(t_lo) > 1:
                    t_lo = [a + b for a, b in zip(t_lo[0::2], t_lo[1::2])]
                    t_hi = [a + b for a, b in zip(t_hi[0::2], t_hi[1::2])]
                acc_lo = acc_lo + t_lo[0]
                acc_hi = acc_hi + t_hi[0]
            y_out[t] = x_ref[t] + jnp.where(lower, acc_lo, pltpu.roll(acc_hi, PACK_ROWS, axis=0))
            return 0

        lax.fori_loop(0, SUBLANES, token, 0)

    def pair(k, _):
        run_group(2 * k, cb_even, cb_odd)
        run_group(2 * k + 1, cb_odd, cb_even)
        return 0

    spread(cb_even, 0, 0, PEER_SLOTS)
    lax.fori_loop(0, n_groups // 2, pair, 0)


def _peer_value(rows, coef, x3, v_packed, tm=256):
    t = x3.shape[0]
    tile = pl.BlockSpec((tm, SUBLANES, LANES), lambda i: (i, 0, 0))
    return pl.pallas_call(
        _peer_value_kernel, name="peer_value",
        grid=(t // tm,),
        in_specs=[_smem_rows(tm), pl.BlockSpec((tm // SUBLANES, PEER_SLOTS, LANES), lambda i: (i, 0, 0)), tile,
                  _table_spec()],
        out_specs=tile,
        out_shape=jax.ShapeDtypeStruct((t, SUBLANES, LANES), F32),
        scratch_shapes=[pltpu.VMEM((SUBLANES * PEER_SLOTS, LANES), F32)] * 2,
        compiler_params=_cparams("arbitrary"),
    )(rows, coef, x3, v_packed)


def _final_norm_kernel(x_ref, g_ref, o_ref):
    o_ref[...] = _rmsnorm(x_ref[...], g_ref[...], D_MODEL)


def _final_norm(x, gain, tm=512):
    t = x.shape[0]
    return pl.pallas_call(
        _final_norm_kernel, name="final_norm",
        grid=(t // tm,),
        in_specs=[pl.BlockSpec((tm, D_MODEL), lambda i: (i, 0)), pl.BlockSpec((1, D_MODEL), lambda i: (0, 0))],
        out_specs=pl.BlockSpec((tm, D_MODEL), lambda i: (i, 0)),
        out_shape=jax.ShapeDtypeStruct((t, D_MODEL), F32),
        compiler_params=_cparams("parallel"),
    )(x, gain)


def _peer_ffn(x1, xn, q, subkeys, u, v):
    t = x1.shape[0]
    ids_h, gate_h = _peer_topk(q, subkeys)
    ids = ids_h.reshape(PEER_SLOTS, t).T
    gate = gate_h.reshape(PEER_SLOTS, t).T
    rows = ids * PACK_ROWS + PACK_ROWS
    upper = (jnp.arange(PEER_SLOTS, dtype=I32) % SUBLANES) >= PACK_ROWS
    rows_u = jnp.where(upper[None, :], rows - PACK_ROWS, rows)
    hid = _peer_hidden(rows_u.reshape(-1), xn.reshape(t, SUBLANES, LANES), _pack_table(u))
    coef = _peer_coef(hid, gate)
    out = _peer_value(rows.reshape(-1), coef, x1.reshape(t, SUBLANES, LANES), _pack_table(v))
    return out.reshape(t, D_MODEL)


def kernel(x, positions, norm_mix, w_in, q_norm, w_uq, kv_norm, w_ukv, conv_w, conv_b, lru_wa, lru_ba, lru_wx, lru_bx,
           lru_lambda, swa_sink, grp_norm, w_out, norm_ffn, peer_wq, peer_subkeys, peer_u, peer_v, norm_final):
    batch, seq, _ = x.shape
    t = batch * seq
    depth = w_in.shape[0]
    xt = x.reshape(t, D_MODEL)

    pos = positions.astype(F32).reshape(t, 1)
    zeros = lambda n: jnp.zeros((n,), F32)
    inv_m, inv_s = _inv_freq(MLA_ROPE), _inv_freq(SWA_HEAD_DIM)
    inv = jnp.concatenate([zeros(MLA_NOPE), inv_m, inv_m, zeros(HP - MLA_NOPE - MLA_ROPE),
                           inv_s, inv_s, zeros(HP - SWA_HEAD_DIM)])
    cos, sin = _rope_tables(pos, inv[None, :])

    for l in range(depth):
        w_in_p, wq, wkv, w_gate, b_gate, ow = _layer_weights(
            l, w_in, w_uq, w_ukv, lru_wa, lru_ba, lru_wx, lru_bx, grp_norm, w_out)
        q, k, v, sq, sk, sv, lx, lg = _in_proj(
            xt, norm_mix[l][None, :], w_in_p, q_norm[l][None, :], wq, kv_norm[l][None, :], wkv, cos, sin)
        o_mla = _mla_attention(q, k, v, batch, seq)
        o_swa = _swa_attention(sq, sk, sv, swa_sink[l][None, :], batch, seq)
        h_f, h_b = _lru_scan(lx, conv_w[l], conv_b[l][None, :], w_gate, b_gate, lru_lambda[l], batch, seq)
        x1, xn, pq = _out_proj(xt, o_mla, h_f, h_b, lg, o_swa, ow, norm_ffn[l][None, :], peer_wq[l].astype(BF16))
        xt = _peer_ffn(x1, xn, pq, peer_subkeys[l], peer_u[l], peer_v[l])
    return _final_norm(xt, norm_final[None, :]).reshape(batch, seq, D_MODEL)
```

```python
import functools

import jax
import jax.numpy as jnp
from jax import lax
from jax.experimental import pallas as pl
from jax.experimental.pallas import tpu as pltpu

F32 = jnp.float32
BF16 = jnp.bfloat16
I32 = jnp.int32

D_MODEL = 1024
MLA_HEADS, MLA_NOPE, MLA_ROPE, MLA_V = 6, 64, 32, 64
MLA_Q_RANK, MLA_KV_RANK = 256, 128
LRU_WIDTH, LRU_C = 384, 8.0
SWA_Q_HEADS, SWA_KV_HEADS, SWA_HEAD_DIM, SWA_WINDOW = 4, 2, 64, 128
ROPE_THETA, EPS, NEG_BIG = 10000.0, 1e-6, -1e30
PEER_HEADS, PEER_NKEYS, PEER_DKEY, PEER_TOPK = 8, 128, 256, 16
PEER_SLOTS = PEER_HEADS * PEER_TOPK
PEER_EXPERTS = PEER_NKEYS * PEER_NKEYS

LANES = 128
SUBLANES = 8
HP = LANES
MLA_QK = MLA_HEADS * HP
SWA_Q = SWA_Q_HEADS * HP
SWA_KV = SWA_KV_HEADS * HP
VMEM_LIMIT = 56 * 1024 * 1024
PACK_ROWS = D_MODEL // (2 * LANES)
TABLE_ROWS = PEER_EXPERTS * PACK_ROWS + 2 * PACK_ROWS

_C_Q, _C_KV, _S_Q, _S_QR, _S_K, _S_KR, _S_V, _L_X, _L_G, _K_R, _IN_COLS = (
    0, 256, 384, 896, 1408, 1664, 1920, 2176, 2560, 2944, 3072)


def _cparams(*sem):
    return pltpu.CompilerParams(dimension_semantics=sem, vmem_limit_bytes=VMEM_LIMIT)


def _rmsnorm(x, g, n):
    return x * lax.rsqrt(jnp.sum(x * x, axis=-1, keepdims=True) * (1.0 / n) + EPS) * g


def _sigmoid(x):
    return 1.0 / (1.0 + jnp.exp(-x))


def _gelu(x):
    return 0.5 * x * (1.0 + lax.erf(x * (2.0 ** -0.5)))


def _rot_half_cols(w, d):
    k = w.shape[0]
    w3 = w.reshape(k, -1, d)
    return jnp.concatenate([-w3[..., d // 2:], w3[..., : d // 2]], axis=-1).reshape(k, -1)


def _pad_heads(w, d, lead=0):
    k = w.shape[0]
    w3 = w.reshape(k, -1, d)
    return jnp.pad(w3, ((0, 0), (0, 0), (lead, HP - d - lead))).reshape(k, -1)


def _inv_freq(d):
    return ROPE_THETA ** (-jnp.arange(0, d, 2, dtype=F32) / d)


def _layer_weights(l, w_in, w_uq, w_ukv, lru_wa, lru_ba, lru_wx, lru_bx, grp_norm, w_out):
    wi = w_in[l]
    c_q, c_kv, k_r = wi[:, 0:256], wi[:, 256:384], wi[:, 384:416]
    s_q, s_k, s_v = wi[:, 416:672], wi[:, 672:800], wi[:, 800:928]
    l_x, l_g = wi[:, 928:1312], wi[:, 1312:1696]
    k_r_blk = jnp.concatenate([jnp.zeros((D_MODEL, MLA_NOPE), F32), k_r, _rot_half_cols(k_r, MLA_ROPE)], axis=1)
    w_in_p = jnp.concatenate([
        c_q, c_kv,
        _pad_heads(s_q, SWA_HEAD_DIM), _pad_heads(_rot_half_cols(s_q, SWA_HEAD_DIM), SWA_HEAD_DIM),
        _pad_heads(s_k, SWA_HEAD_DIM), _pad_heads(_rot_half_cols(s_k, SWA_HEAD_DIM), SWA_HEAD_DIM),
        _pad_heads(s_v, SWA_HEAD_DIM), l_x, l_g, k_r_blk], axis=1).astype(BF16)

    wq3 = w_uq[l].reshape(MLA_Q_RANK, MLA_HEADS, MLA_NOPE + MLA_ROPE)
    q_nope = wq3[..., :MLA_NOPE].reshape(MLA_Q_RANK, -1)
    q_rope = wq3[..., MLA_NOPE:].reshape(MLA_Q_RANK, -1)
    wq = jnp.concatenate([
        _pad_heads(q_nope, MLA_NOPE) + _pad_heads(q_rope, MLA_ROPE, lead=MLA_NOPE),
        _pad_heads(_rot_half_cols(q_rope, MLA_ROPE), MLA_ROPE, lead=MLA_NOPE)], axis=1).astype(BF16)

    wkv3 = w_ukv[l].reshape(MLA_KV_RANK, MLA_HEADS, MLA_NOPE + MLA_V)
    wkv = jnp.concatenate([
        _pad_heads(wkv3[..., :MLA_NOPE].reshape(MLA_KV_RANK, -1), MLA_NOPE),
        _pad_heads(wkv3[..., MLA_NOPE:].reshape(MLA_KV_RANK, -1), MLA_V)], axis=1).astype(BF16)

    def blockdiag(w):
        n, bi, bj = w.shape
        eye = jnp.eye(n, dtype=w.dtype)
        return (w[:, :, None, :] * eye[:, None, :, None]).reshape(n * bi, n * bj)

    w_gate = [jnp.concatenate([blockdiag(lru_wa[l, d]), blockdiag(lru_wx[l, d])], axis=1).astype(BF16) for d in range(2)]
    b_gate = [jnp.concatenate([lru_ba[l, d].reshape(1, -1), lru_bx[l, d].reshape(1, -1)], axis=1) for d in range(2)]

    g = grp_norm[l]
    wo = w_out[l]
    ga, gb = MLA_HEADS * MLA_V, MLA_HEADS * MLA_V + LRU_WIDTH

    def pad_rows(w, d):
        return jnp.pad(w.reshape(-1, d, w.shape[-1]), ((0, 0), (0, HP - d), (0, 0))).reshape(-1, w.shape[-1])

    out_w = dict(
        g_a=_pad_heads(g[None, :ga], MLA_V), g_b=g[None, ga:gb], g_c=_pad_heads(g[None, gb:], SWA_HEAD_DIM),
        w_a=pad_rows(wo[:ga], MLA_V).astype(BF16), w_b=wo[ga:gb].astype(BF16),
        w_c=pad_rows(wo[gb:], SWA_HEAD_DIM).astype(BF16))
    return w_in_p, wq, wkv, w_gate, b_gate, out_w


def _rope_table_kernel(pos_ref, inv_ref, cos_ref, sin_ref):
    ang = pos_ref[...] * inv_ref[...]
    cos_ref[...] = jnp.cos(ang)
    sin_ref[...] = jnp.sin(ang)


def _rope_tables(pos, inv, tm=512):
    t, w = pos.shape[0], inv.shape[1]
    return pl.pallas_call(
        _rope_table_kernel, name="rope_tables",
        grid=(t // tm,),
        in_specs=[pl.BlockSpec((tm, 1), lambda i: (i, 0)), pl.BlockSpec((1, w), lambda i: (0, 0))],
        out_specs=[pl.BlockSpec((tm, w), lambda i: (i, 0))] * 2,
        out_shape=[jax.ShapeDtypeStruct((t, w), F32)] * 2,
        compiler_params=_cparams("parallel"),
    )(pos, inv)


def _in_proj_kernel(x_ref, nmix_ref, win_ref, qn_ref, wq_ref, kvn_ref, wkv_ref, cos_ref, sin_ref,
                    q_out, k_out, v_out, sq_out, sk_out, sv_out, lx_out, lg_out):
    xn = _rmsnorm(x_ref[...], nmix_ref[...], D_MODEL)
    proj = jnp.dot(xn.astype(BF16), win_ref[...], preferred_element_type=F32)
    cos, sin = cos_ref[...], sin_ref[...]
    cosq = jnp.concatenate([cos[:, :HP]] * MLA_HEADS, axis=1)
    sinq = jnp.concatenate([sin[:, :HP]] * MLA_HEADS, axis=1)
    coss = jnp.concatenate([cos[:, HP:]] * SWA_Q_HEADS, axis=1)
    sins = jnp.concatenate([sin[:, HP:]] * SWA_Q_HEADS, axis=1)

    swa_scale = SWA_HEAD_DIM ** -0.5
    sq_out[...] = ((proj[:, _S_Q:_S_QR] * coss + proj[:, _S_QR:_S_K] * sins) * swa_scale).astype(BF16)
    sk_out[...] = (proj[:, _S_K:_S_KR] * coss[:, :SWA_KV] + proj[:, _S_KR:_S_V] * sins[:, :SWA_KV]).astype(BF16)
    sv_out[...] = proj[:, _S_V:_L_X].astype(BF16)
    lx_out[...] = proj[:, _L_X:_L_G]
    lg_out[...] = proj[:, _L_G:_K_R]

    cqn = _rmsnorm(proj[:, _C_Q:_C_KV], qn_ref[...], MLA_Q_RANK).astype(BF16)
    qq = jnp.dot(cqn, wq_ref[...], preferred_element_type=F32)
    mla_scale = (MLA_NOPE + MLA_ROPE) ** -0.5
    q_out[...] = ((qq[:, :MLA_QK] * cosq + qq[:, MLA_QK:] * sinq) * mla_scale).astype(BF16)

    kvn = _rmsnorm(proj[:, _C_KV:_S_Q], kvn_ref[...], MLA_KV_RANK).astype(BF16)
    kv = jnp.dot(kvn, wkv_ref[...], preferred_element_type=F32)
    kr = proj[:, _K_R:_IN_COLS]
    lane = lax.broadcasted_iota(I32, kr.shape, 1)
    in_rope = (lane >= MLA_NOPE) & (lane < MLA_NOPE + MLA_ROPE)
    kr_rot = jnp.where(in_rope, kr * cosq[:, :HP] + pltpu.roll(kr, HP - MLA_ROPE, axis=1) * sinq[:, :HP], 0.0)
    for h in range(MLA_HEADS):
        k_out[:, h * HP:(h + 1) * HP] = (kv[:, h * HP:(h + 1) * HP] + kr_rot).astype(BF16)
    vlane = lax.broadcasted_iota(I32, (kv.shape[0], MLA_QK), 1)
    v_out[...] = jnp.where(vlane % HP == MLA_V, 1.0, kv[:, MLA_QK:]).astype(BF16)


def _in_proj(x, nmix, w_in_p, qn, wq, kvn, wkv, cos, sin, tm=256):
    t = x.shape[0]
    row = lambda w: pl.BlockSpec((tm, w), lambda i: (i, 0))
    full = lambda a: pl.BlockSpec(a.shape, lambda i: (0,) * a.ndim)
    widths = (MLA_QK, MLA_QK, MLA_QK, SWA_Q, SWA_KV, SWA_KV, LRU_WIDTH, LRU_WIDTH)
    dtypes = (BF16, BF16, BF16, BF16, BF16, BF16, F32, F32)
    return pl.pallas_call(
        _in_proj_kernel, name="in_proj",
        grid=(t // tm,),
        in_specs=[row(D_MODEL), full(nmix), full(w_in_p), full(qn), full(wq), full(kvn), full(wkv),
                  row(2 * HP), row(2 * HP)],
        out_specs=[row(w) for w in widths],
        out_shape=[jax.ShapeDtypeStruct((t, w), d) for w, d in zip(widths, dtypes)],
        compiler_params=_cparams("parallel"),
    )(x, nmix, w_in_p, qn, wq, kvn, wkv, cos, sin)


def _mla_kernel(q_ref, k_ref, v_ref, o_ref, *, tk):
    q = q_ref[...]
    tq = q.shape[0]
    nk = k_ref.shape[0] // tk

    def body(j, carry):
        m, acc = carry
        off = pl.multiple_of(j * tk, tk)
        k = k_ref[pl.ds(off, tk), :]
        v = v_ref[pl.ds(off, tk), :]
        s = lax.dot_general(q, k, (((1,), (1,)), ((), ())), preferred_element_type=F32)
        m_new = jnp.maximum(m, jnp.max(s, axis=-1, keepdims=True))
        p = jnp.exp(s - m_new)
        acc = jnp.exp(m - m_new) * acc + jnp.dot(p.astype(BF16), v, preferred_element_type=F32)
        return m_new, acc

    init = (jnp.full((tq, 1), -jnp.inf, F32), jnp.zeros((tq, HP), F32))
    _, acc = lax.fori_loop(0, nk, body, init)
    lane = lax.broadcasted_iota(I32, acc.shape, 1)
    o_ref[...] = jnp.where(lane < MLA_V, acc / acc[:, MLA_V:MLA_V + 1], 0.0)


def _mla_attention(q, k, v, batch, seq, tq=4096, tk=1024):
    tq, tk = min(tq, seq), min(tk, seq)
    nq = seq // tq
    return pl.pallas_call(
        functools.partial(_mla_kernel, tk=tk), name="mla_attention",
        grid=(batch, MLA_HEADS, nq),
        in_specs=[pl.BlockSpec((tq, HP), lambda b, h, i: (b * nq + i, h)),
                  pl.BlockSpec((seq, HP), lambda b, h, i: (b, h)),
                  pl.BlockSpec((seq, HP), lambda b, h, i: (b, h))],
        out_specs=pl.BlockSpec((tq, HP), lambda b, h, i: (b * nq + i, h)),
        out_shape=jax.ShapeDtypeStruct((batch * seq, MLA_QK), F32),
        compiler_params=_cparams("parallel", "parallel", "arbitrary"),
    )(q, k, v)


def _swa_kernel(sink_ref, q_ref, kp_ref, kc_ref, kn_ref, vp_ref, vc_ref, vn_ref, o_ref, *, seq):
    w = SWA_WINDOW
    tq = q_ref.shape[0]
    i = pl.program_id(1)
    kcat = jnp.concatenate([kp_ref[...], kc_ref[...], kn_ref[...]], axis=0)
    vcat = jnp.concatenate([vp_ref[...], vc_ref[...], vn_ref[...]], axis=0)
    for j in range(tq // w):
        qpos = i * tq + j * w + lax.broadcasted_iota(I32, (w, 3 * w), 0)
        kpos = i * tq + (j - 1) * w + lax.broadcasted_iota(I32, (w, 3 * w), 1)
        valid = (jnp.abs(kpos - qpos) <= w) & (kpos >= 0) & (kpos < seq)
        for h in range(SWA_Q_HEADS):
            kh = h // (SWA_Q_HEADS // SWA_KV_HEADS)
            qh = q_ref[j * w:(j + 1) * w, h * HP:(h + 1) * HP]
            kj = kcat[j * w:(j + 3) * w, kh * HP:(kh + 1) * HP]
            vj = vcat[j * w:(j + 3) * w, kh * HP:(kh + 1) * HP]
            s = lax.dot_general(qh, kj, (((1,), (1,)), ((), ())), preferred_element_type=F32)
            s = jnp.where(valid, s, NEG_BIG)
            sink = sink_ref[0, h]
            m = jnp.maximum(jnp.max(s, axis=-1, keepdims=True), sink)
            p = jnp.exp(s - m)
            den = jnp.sum(p, axis=-1, keepdims=True) + jnp.exp(sink - m)
            o_ref[j * w:(j + 1) * w, h * HP:(h + 1) * HP] = jnp.dot(p.astype(BF16), vj, preferred_element_type=F32) / den


def _swa_attention(sq, sk, sv, sink, batch, seq, tq=512):
    w = SWA_WINDOW
    nq, nw, r = seq // tq, seq // w, tq // w
    prev = pl.BlockSpec((w, SWA_KV), lambda b, i: (b * nw + jnp.maximum(i * r - 1, 0), 0))
    cur = pl.BlockSpec((tq, SWA_KV), lambda b, i: (b * nq + i, 0))
    nxt = pl.BlockSpec((w, SWA_KV), lambda b, i: (b * nw + jnp.minimum((i + 1) * r, nw - 1), 0))
    return pl.pallas_call(
        functools.partial(_swa_kernel, seq=seq), name="swa_attention",
        grid=(batch, nq),
        in_specs=[pl.BlockSpec(memory_space=pltpu.SMEM),
                  pl.BlockSpec((tq, SWA_Q), lambda b, i: (b * nq + i, 0)),
                  prev, cur, nxt, prev, cur, nxt],
        out_specs=pl.BlockSpec((tq, SWA_Q), lambda b, i: (b * nq + i, 0)),
        out_shape=jax.ShapeDtypeStruct((batch * seq, SWA_Q), F32),
        compiler_params=_cparams("parallel", "parallel"),
    )(sink, sq, sk, sk, sk, sv, sv, sv)


def _lru_kernel(xpf_ref, xcf_ref, xnf_ref, xpb_ref, xcb_ref, xnb_ref, cw_ref, cb_ref, wf_ref, bf_ref, wb_ref, bb_ref,
                lam_ref, hf_out, hb_out, carry_f, carry_b, *, nt):
    tm = xcf_ref.shape[0]
    i = pl.program_id(1)
    halo = xpf_ref.shape[0]

    @pl.when(i == 0)
    def _():
        carry_f[...] = jnp.zeros_like(carry_f)
        carry_b[...] = jnp.zeros_like(carry_b)

    def gates(xp_ref, xc_ref, xn_ref, first, last, w_ref, b_ref, lam):
        prev = jnp.where(first, 0.0, xp_ref[...])
        nxt = jnp.where(last, 0.0, xn_ref[...])
        xcat = jnp.concatenate([prev, xc_ref[...], nxt], axis=0)
        cw = cw_ref[...]
        conv = cb_ref[...]
        for tap in range(cw.shape[0]):
            conv = conv + cw[tap:tap + 1, :] * xcat[halo - 1 + tap: halo - 1 + tap + tm, :]
        g = jnp.dot(conv.astype(BF16), w_ref[...], preferred_element_type=F32) + b_ref[...]
        r = _sigmoid(g[:, :LRU_WIDTH])
        gate_i = _sigmoid(g[:, LRU_WIDTH:])
        softplus = jnp.maximum(-lam, 0.0) + jnp.log1p(jnp.exp(-jnp.abs(lam)))
        log_a = -LRU_C * r * softplus
        a = jnp.exp(log_a)
        b = jnp.sqrt(1.0 - a * a) * (gate_i * conv)
        return a, b

    row = lax.broadcasted_iota(I32, (tm, LRU_WIDTH), 0)

    def scan(a, b, reverse):
        k = 1
        while k < tm:
            if reverse:
                keep = row < tm - k
                shift = tm - k
            else:
                keep = row >= k
                shift = k
            a_s = jnp.where(keep, pltpu.roll(a, shift, axis=0), 1.0)
            b_s = jnp.where(keep, pltpu.roll(b, shift, axis=0), 0.0)
            b = a * b_s + b
            a = a * a_s
            k *= 2
        return a, b

    a, b = gates(xpf_ref, xcf_ref, xnf_ref, i == 0, i == nt - 1, wf_ref, bf_ref, lam_ref[0:1, :])
    a, b = scan(a, b, False)
    h = a * carry_f[...] + b
    hf_out[...] = h
    carry_f[...] = h[tm - 1:tm, :]

    a, b = gates(xpb_ref, xcb_ref, xnb_ref, i == nt - 1, i == 0, wb_ref, bb_ref, lam_ref[1:2, :])
    a, b = scan(a, b, True)
    h = a * carry_b[...] + b
    hb_out[...] = h
    carry_b[...] = h[0:1, :]


def _lru_scan(lx, conv_w, conv_b, w_gate, b_gate, lam, batch, seq, tm=256):
    nt, hb = seq // tm, seq // SUBLANES
    r = tm // SUBLANES
    fwd = lambda b, i: i
    bwd = lambda b, i: nt - 1 - i

    def specs(tile):
        return [pl.BlockSpec((SUBLANES, LRU_WIDTH), lambda b, i: (b * hb + jnp.maximum(tile(b, i) * r - 1, 0), 0)),
                pl.BlockSpec((tm, LRU_WIDTH), lambda b, i: (b * nt + tile(b, i), 0)),
                pl.BlockSpec((SUBLANES, LRU_WIDTH), lambda b, i: (b * hb + jnp.minimum((tile(b, i) + 1) * r, hb - 1), 0))]

    full = lambda a: pl.BlockSpec(a.shape, lambda b, i: (0,) * a.ndim)
    return pl.pallas_call(
        functools.partial(_lru_kernel, nt=nt), name="lru_scan",
        grid=(batch, nt),
        in_specs=specs(fwd) + specs(bwd) + [full(conv_w), full(conv_b), full(w_gate[0]), full(b_gate[0]),
                                            full(w_gate[1]), full(b_gate[1]), full(lam)],
        out_specs=[pl.BlockSpec((tm, LRU_WIDTH), lambda b, i: (b * nt + i, 0)),
                   pl.BlockSpec((tm, LRU_WIDTH), lambda b, i: (b * nt + nt - 1 - i, 0))],
        out_shape=[jax.ShapeDtypeStruct((batch * seq, LRU_WIDTH), F32)] * 2,
        scratch_shapes=[pltpu.VMEM((1, LRU_WIDTH), F32), pltpu.VMEM((1, LRU_WIDTH), F32)],
        compiler_params=_cparams("parallel", "arbitrary"),
    )(lx, lx, lx, lx, lx, lx, conv_w, conv_b, w_gate[0], b_gate[0], w_gate[1], b_gate[1], lam)


def _out_proj_kernel(x_ref, oa_ref, hf_ref, hb_ref, lg_ref, oc_ref, ga_ref, gb_ref, gc_ref, wa_ref, wb_ref, wc_ref,
                     nffn_ref, wq_ref, x1_out, xn_out, q_out):
    mix_a = _rmsnorm(oa_ref[...], ga_ref[...], MLA_HEADS * MLA_V).astype(BF16)
    o_lru = (hf_ref[...] + hb_ref[...]) * _gelu(lg_ref[...])
    mix_b = _rmsnorm(o_lru, gb_ref[...], LRU_WIDTH).astype(BF16)
    mix_c = _rmsnorm(oc_ref[...], gc_ref[...], SWA_Q_HEADS * SWA_HEAD_DIM).astype(BF16)
    x1 = (x_ref[...]
          + jnp.dot(mix_a, wa_ref[...], preferred_element_type=F32)
          + jnp.dot(mix_b, wb_ref[...], preferred_element_type=F32)
          + jnp.dot(mix_c, wc_ref[...], preferred_element_type=F32))
    x1_out[...] = x1
    xn = _rmsnorm(x1, nffn_ref[...], D_MODEL)
    xn_out[...] = xn
    q_out[...] = jnp.dot(xn.astype(BF16), wq_ref[...], preferred_element_type=F32)


def _out_proj(x, o_mla, h_f, h_b, l_g, o_swa, ow, nffn, wq, tm=256):
    t = x.shape[0]
    row = lambda w: pl.BlockSpec((tm, w), lambda i: (i, 0))
    full = lambda a: pl.BlockSpec(a.shape, lambda i: (0,) * a.ndim)
    nq = wq.shape[1]
    return pl.pallas_call(
        _out_proj_kernel, name="out_proj",
        grid=(t // tm,),
        in_specs=[row(D_MODEL), row(MLA_QK), row(LRU_WIDTH), row(LRU_WIDTH), row(LRU_WIDTH), row(SWA_Q),
                  full(ow["g_a"]), full(ow["g_b"]), full(ow["g_c"]), full(ow["w_a"]), full(ow["w_b"]), full(ow["w_c"]),
                  full(nffn), full(wq)],
        out_specs=[row(D_MODEL), row(D_MODEL), row(nq)],
        out_shape=[jax.ShapeDtypeStruct((t, D_MODEL), F32), jax.ShapeDtypeStruct((t, D_MODEL), F32),
                   jax.ShapeDtypeStruct((t, nq), F32)],
        compiler_params=_cparams("parallel"),
    )(x, o_mla, h_f, h_b, l_g, o_swa, ow["g_a"], ow["g_b"], ow["g_c"], ow["w_a"], ow["w_b"], ow["w_c"], nffn, wq)


_INT_MAX = 2 ** 31 - 1


def _top_k_rows(s, k, tag=None):
    if tag is None:
        tag = lax.broadcasted_iota(I32, s.shape, 0)
    vals, picks = [], []
    for _ in range(k):
        m = jnp.max(s, axis=0, keepdims=True)
        idx = jnp.min(jnp.where(s == m, tag, _INT_MAX), axis=0, keepdims=True)
        vals.append(m)
        picks.append(idx)
        s = jnp.where(tag == idx, -jnp.inf, s)
    return jnp.concatenate(vals, axis=0), jnp.concatenate(picks, axis=0)


def _candidates(v1, v2):
    k, tm = v1.shape
    row8 = lax.broadcasted_iota(I32, (SUBLANES, tm), 0)
    row16 = lax.broadcasted_iota(I32, (k, tm), 0)
    sums = [v1[0:1] + v2, v1[1:2] + v2[0:SUBLANES], v1[SUBLANES:] + v2[0:1]]
    tags = [row16, k + row8, (row8 + SUBLANES) * k]
    mid = 3 * SUBLANES
    rowm = lax.broadcasted_iota(I32, (mid, tm), 0)
    arow = jnp.full((mid, tm), -1, I32)
    brow = jnp.zeros((mid, tm), I32)
    start = 0
    for a in range(2, SUBLANES):
        nb = k // (a + 1)
        inrun = (rowm >= start) & (rowm < start + nb)
        arow = jnp.where(inrun, a, arow)
        brow = jnp.where(inrun, rowm - start, brow)
        start += nb
    part_a = jnp.zeros((mid, tm), F32)
    part_b = jnp.zeros((mid, tm), F32)
    for a in range(2, SUBLANES):
        part_a = jnp.where(arow == a, v1[a:a + 1], part_a)
    for b in range(k // 3):
        part_b = jnp.where(brow == b, v2[b:b + 1], part_b)
    used = arow >= 0
    sums.append(jnp.where(used, part_a + part_b, -jnp.inf))
    tags.append(jnp.where(used, arow * k + brow, _INT_MAX))
    return jnp.concatenate(sums, axis=0), jnp.concatenate(tags, axis=0)


def _take_rows(table, idx):
    out = jnp.zeros(idx.shape, table.dtype)
    for a in range(table.shape[0]):
        out = jnp.where(idx == a, table[a:a + 1], out)
    return out


def _peer_topk_kernel(q_ref, keys_ref, id_out, gate_out):
    half = PEER_DKEY // 2
    dn = (((1,), (1,)), ((), ()))
    q = q_ref[...]
    s1 = lax.dot_general(keys_ref[0], q[:, :half], dn, preferred_element_type=F32, precision=lax.Precision.HIGHEST)
    s2 = lax.dot_general(keys_ref[1], q[:, half:], dn, preferred_element_type=F32, precision=lax.Precision.HIGHEST)
    v1, i1 = _top_k_rows(s1, PEER_TOPK)
    v2, i2 = _top_k_rows(s2, PEER_TOPK)
    cand, flat = _candidates(v1, v2)
    sc, pick = _top_k_rows(cand, PEER_TOPK, tag=flat)
    e = jnp.exp(sc - sc[0:1, :])
    rank1, rank2 = pick >> 4, pick & (PEER_TOPK - 1)
    id_out[0] = _take_rows(i1, rank1) * PEER_NKEYS + _take_rows(i2, rank2)
    gate_out[0] = e / jnp.sum(e, axis=0, keepdims=True)


def _peer_topk(q, subkeys, tm=1024):
    t = q.shape[0]
    tm = min(tm, t)
    out = pl.BlockSpec((1, PEER_TOPK, tm), lambda i, h: (h, 0, i))
    return pl.pallas_call(
        _peer_topk_kernel, name="peer_topk",
        grid=(t // tm, PEER_HEADS),
        in_specs=[pl.BlockSpec((tm, PEER_DKEY), lambda i, h: (i, h)),
                  pl.BlockSpec(subkeys.shape, lambda i, h: (0, 0, 0))],
        out_specs=[out, out],
        out_shape=[jax.ShapeDtypeStruct((PEER_HEADS, PEER_TOPK, t), I32),
                   jax.ShapeDtypeStruct((PEER_HEADS, PEER_TOPK, t), F32)],
        compiler_params=_cparams("parallel", "parallel"),
    )(q, subkeys)


def _fold_halves(vals, sub):
    even = sub % 2 == 0
    first_two = sub % 4 < 2
    z0 = jnp.where(even, vals[0] + pltpu.roll(vals[0], 7, axis=0), vals[1] + pltpu.roll(vals[1], 1, axis=0))
    z1 = jnp.where(even, vals[2] + pltpu.roll(vals[2], 7, axis=0), vals[3] + pltpu.roll(vals[3], 1, axis=0))
    return jnp.where(first_two, z0 + pltpu.roll(z0, 6, axis=0), z1 + pltpu.roll(z1, 2, axis=0))


def _pack_table(w):
    bits = lax.bitcast_convert_type(w.astype(BF16), jnp.uint16).astype(jnp.uint32)
    half = D_MODEL // 2
    words = lax.bitcast_convert_type((bits[:, half:] << 16) | bits[:, :half], I32)
    return jnp.pad(words.reshape(-1, LANES), ((PACK_ROWS, PACK_ROWS), (0, 0)))


def _unpack(w):
    return pltpu.bitcast(w << 16, F32), pltpu.bitcast(w & -0x10000, F32)


def _peer_hidden_kernel(id_ref, x_ref, u_ref, h_out, r_even, r_odd):
    tm = x_ref.shape[0]
    n_groups = tm // SUBLANES
    sub = lax.broadcasted_iota(I32, (SUBLANES, LANES), 0)
    lower = sub < PACK_ROWS
    groups = PEER_SLOTS // SUBLANES
    ones = jnp.ones((SUBLANES, 2 * LANES), BF16)
    dn = (((1,), (1,)), ((), ()))
    head = 5

    def token(tt, g, r_scr):
        t = g * SUBLANES + tt
        xt = x_ref[t]
        x_lo = jnp.where(lower, xt, pltpu.roll(xt, PACK_ROWS, axis=0))
        x_hi = jnp.where(lower, pltpu.roll(xt, PACK_ROWS, axis=0), xt)
        for j in range(groups):
            prods = []
            for i in range(PACK_ROWS):
                base = t * PEER_SLOTS + j * SUBLANES + i
                w_a = u_ref[pl.ds(pl.multiple_of(id_ref[base], PACK_ROWS), SUBLANES), :]
                w_b = u_ref[pl.ds(pl.multiple_of(id_ref[base + PACK_ROWS], PACK_ROWS), SUBLANES), :]
                lo, hi = _unpack(jnp.where(lower, w_a, w_b))
                prods.append(lo * x_lo + hi * x_hi)
            r_scr[pl.ds(pl.multiple_of(tt * PEER_SLOTS + j * SUBLANES, SUBLANES), SUBLANES), :] = _fold_halves(prods, sub)

    def lane_sums(r_scr, g):
        r = r_scr[...]
        hi = r.astype(BF16)
        lo = (r - hi.astype(F32)).astype(BF16)
        sums = lax.dot_general(ones, jnp.concatenate([hi, lo], axis=1), dn, preferred_element_type=F32)
        out = jnp.zeros((SUBLANES, LANES), F32)
        for tt in range(SUBLANES):
            out = jnp.where(sub == tt, sums[:, tt * PEER_SLOTS:(tt + 1) * PEER_SLOTS], out)
        h_out[pl.ds(pl.multiple_of(g * SUBLANES, SUBLANES), SUBLANES), :] = out

    def run_group(g, r_mine, r_prev):
        lane_sums(r_prev, jnp.maximum(g - 1, 0))
        for tt in range(head):
            token(tt, g, r_mine)

        def rest(tt, _):
            token(tt, g, r_mine)
            return 0

        lax.fori_loop(head, SUBLANES, rest, 0)

    def pair(k, _):
        run_group(2 * k, r_even, r_odd)
        run_group(2 * k + 1, r_odd, r_even)
        return 0

    @pl.when(pl.program_id(0) == 0)
    def _():
        r_odd[...] = jnp.zeros_like(r_odd)

    lax.fori_loop(0, n_groups // 2, pair, 0)
    lane_sums(r_odd, n_groups - 1)


def _smem_rows(tm):
    return pl.BlockSpec((tm * PEER_SLOTS,), lambda i: (i,), memory_space=pltpu.SMEM)


def _table_spec():
    return pl.BlockSpec((TABLE_ROWS, LANES), lambda i: (0, 0), pipeline_mode=pl.Buffered(1))


def _peer_hidden(rows, x3, u_packed, tm=256):
    t = x3.shape[0]
    return pl.pallas_call(
        _peer_hidden_kernel, name="peer_hidden",
        grid=(t // tm,),
        in_specs=[_smem_rows(tm), pl.BlockSpec((tm, SUBLANES, LANES), lambda i: (i, 0, 0)), _table_spec()],
        out_specs=pl.BlockSpec((tm, PEER_SLOTS), lambda i: (i, 0)),
        out_shape=jax.ShapeDtypeStruct((t, PEER_SLOTS), F32),
        scratch_shapes=[pltpu.VMEM((SUBLANES * PEER_SLOTS, LANES), F32)] * 2,
        compiler_params=_cparams("arbitrary"),
    )(rows, x3, u_packed)


def _peer_coef_kernel(h_ref, gate_ref, c_out):
    ct = (gate_ref[...] * _gelu(h_ref[...])).T
    for g in range(ct.shape[1] // SUBLANES):
        c_out[g] = ct if g == 0 else pltpu.roll(ct, LANES - SUBLANES * g, axis=1)


def _peer_coef(hid, gate, tm=LANES):
    t = gate.shape[0]
    return pl.pallas_call(
        _peer_coef_kernel, name="peer_coef",
        grid=(t // tm,),
        in_specs=[pl.BlockSpec((tm, PEER_SLOTS), lambda i: (i, 0)), pl.BlockSpec((tm, PEER_SLOTS), lambda i: (i, 0))],
        out_specs=pl.BlockSpec((tm // SUBLANES, PEER_SLOTS, LANES), lambda i: (i, 0, 0)),
        out_shape=jax.ShapeDtypeStruct((t // SUBLANES, PEER_SLOTS, LANES), F32),
        compiler_params=_cparams("parallel"),
    )(hid, gate)


def _peer_value_kernel(id_ref, c_ref, x_ref, v_ref, y_out, cb_even, cb_odd):
    tm = y_out.shape[0]
    n_groups = tm // SUBLANES
    lower = lax.broadcasted_iota(I32, (SUBLANES, LANES), 0) < PACK_ROWS

    def spread(cb, g, e0, ne):
        rows = c_ref[g, pl.ds(e0, ne), :]
        for n in range(SUBLANES):
            cb[pl.ds(n * PEER_SLOTS + e0, ne), :] = jnp.broadcast_to(rows[:, n:n + 1], (ne, LANES))

    def run_group(g, cb_use, cb_fill):
        g_next = jnp.minimum(g + 1, n_groups - 1)
        per_token = PEER_SLOTS // SUBLANES

        def token(tt, _):
            t = g * SUBLANES + tt
            spread(cb_fill, g_next, pl.multiple_of(tt * per_token, per_token), per_token)
            cb_scr = cb_use

            acc_lo = jnp.zeros((SUBLANES, LANES), F32)
            acc_hi = jnp.zeros((SUBLANES, LANES), F32)
            for j in range(PEER_SLOTS // SUBLANES):
                t_lo, t_hi = [], []
                for s in range(SUBLANES):
                    e = j * SUBLANES + s
                    row = pl.multiple_of(id_ref[t * PEER_SLOTS + e], PACK_ROWS)
                    c = jnp.broadcast_to(cb_scr[pl.ds(tt * PEER_SLOTS + e, 1), :], (SUBLANES, LANES))
                    lo, hi = _unpack(v_ref[pl.ds(row, SUBLANES), :])
                    t_lo.append(c * lo)
                    t_hi.append(c * hi)
                while len(t_lo) > 1:
                    t_lo = [a + b for a, b in zip(t_lo[0::2], t_lo[1::2])]
                    t_hi = [a + b for a, b in zip(t_hi[0::2], t_hi[1::2])]
                acc_lo = acc_lo + t_lo[0]
                acc_hi = acc_hi + t_hi[0]
            y_out[t] = x_ref[t] + jnp.where(lower, acc_lo, pltpu.roll(acc_hi, PACK_ROWS, axis=0))
            return 0

        lax.fori_loop(0, SUBLANES, token, 0)

    def pair(k, _):
        run_group(2 * k, cb_even, cb_odd)
        run_group(2 * k + 1, cb_odd, cb_even)
        return 0

    spread(cb_even, 0, 0, PEER_SLOTS)
    lax.fori_loop(0, n_groups // 2, pair, 0)


def _peer_value(rows, coef, x3, v_packed, tm=256):
    t = x3.shape[0]
    tile = pl.BlockSpec((tm, SUBLANES, LANES), lambda i: (i, 0, 0))
    return pl.pallas_call(
        _peer_value_kernel, name="peer_value",
        grid=(t // tm,),
        in_specs=[_smem_rows(tm), pl.BlockSpec((tm // SUBLANES, PEER_SLOTS, LANES), lambda i: (i, 0, 0)), tile,
                  _table_spec()],
        out_specs=tile,
        out_shape=jax.ShapeDtypeStruct((t, SUBLANES, LANES), F32),
        scratch_shapes=[pltpu.VMEM((SUBLANES * PEER_SLOTS, LANES), F32)] * 2,
        compiler_params=_cparams("arbitrary"),
    )(rows, coef, x3, v_packed)


def _final_norm_kernel(x_ref, g_ref, o_ref):
    o_ref[...] = _rmsnorm(x_ref[...], g_ref[...], D_MODEL)


def _final_norm(x, gain, tm=512):
    t = x.shape[0]
    return pl.pallas_call(
        _final_norm_kernel, name="final_norm",
        grid=(t // tm,),
        in_specs=[pl.BlockSpec((tm, D_MODEL), lambda i: (i, 0)), pl.BlockSpec((1, D_MODEL), lambda i: (0, 0))],
        out_specs=pl.BlockSpec((tm, D_MODEL), lambda i: (i, 0)),
        out_shape=jax.ShapeDtypeStruct((t, D_MODEL), F32),
        compiler_params=_cparams("parallel"),
    )(x, gain)


def _peer_ffn(x1, xn, q, subkeys, u, v):
    t = x1.shape[0]
    ids_h, gate_h = _peer_topk(q, subkeys)
    ids = ids_h.reshape(PEER_SLOTS, t).T
    gate = gate_h.reshape(PEER_SLOTS, t).T
    rows = ids * PACK_ROWS + PACK_ROWS
    upper = (jnp.arange(PEER_SLOTS, dtype=I32) % SUBLANES) >= PACK_ROWS
    rows_u = jnp.where(upper[None, :], rows - PACK_ROWS, rows)
    hid = _peer_hidden(rows_u.reshape(-1), xn.reshape(t, SUBLANES, LANES), _pack_table(u))
    coef = _peer_coef(hid, gate)
    out = _peer_value(rows.reshape(-1), coef, x1.reshape(t, SUBLANES, LANES), _pack_table(v))
    return out.reshape(t, D_MODEL)


def kernel(x, positions, norm_mix, w_in, q_norm, w_uq, kv_norm, w_ukv, conv_w, conv_b, lru_wa, lru_ba, lru_wx, lru_bx,
           lru_lambda, swa_sink, grp_norm, w_out, norm_ffn, peer_wq, peer_subkeys, peer_u, peer_v, norm_final):
    batch, seq, _ = x.shape
    t = batch * seq
    depth = w_in.shape[0]
    xt = x.reshape(t, D_MODEL)

    pos = positions.astype(F32).reshape(t, 1)
    zeros = lambda n: jnp.zeros((n,), F32)
    inv_m, inv_s = _inv_freq(MLA_ROPE), _inv_freq(SWA_HEAD_DIM)
    inv = jnp.concatenate([zeros(MLA_NOPE), inv_m, inv_m, zeros(HP - MLA_NOPE - MLA_ROPE),
                           inv_s, inv_s, zeros(HP - SWA_HEAD_DIM)])
    cos, sin = _rope_tables(pos, inv[None, :])

    for l in range(depth):
        w_in_p, wq, wkv, w_gate, b_gate, ow = _layer_weights(
            l, w_in, w_uq, w_ukv, lru_wa, lru_ba, lru_wx, lru_bx, grp_norm, w_out)
        q, k, v, sq, sk, sv, lx, lg = _in_proj(
            xt, norm_mix[l][None, :], w_in_p, q_norm[l][None, :], wq, kv_norm[l][None, :], wkv, cos, sin)
        o_mla = _mla_attention(q, k, v, batch, seq)
        o_swa = _swa_attention(sq, sk, sv, swa_sink[l][None, :], batch, seq)
        h_f, h_b = _lru_scan(lx, conv_w[l], conv_b[l][None, :], w_gate, b_gate, lru_lambda[l], batch, seq)
        x1, xn, pq = _out_proj(xt, o_mla, h_f, h_b, lg, o_swa, ow, norm_ffn[l][None, :], peer_wq[l].astype(BF16))
        xt = _peer_ffn(x1, xn, pq, peer_subkeys[l], peer_u[l], peer_v[l])
    return _final_norm(xt, norm_final[None, :]).reshape(batch, seq, D_MODEL)
```
